```python
import jax
import jax.numpy as jnp
from jax import lax
import numpy as np

D_MODEL = 1024
BATCH = 2
SEQ = 8192
DEPTH = 1

CHUNK = 64
N_META = 16
Q_BLOCK = 128

A_HEADS = 8
A_HEAD_DIM = 64
A_KV_RANK = 128
IDX_HEADS = 8
IDX_DIM = 64
TOPK_MAX = 256

B_HEADS = 8
B_HEAD_DIM = 64
DECAY_LORA = 64
AAA_LORA = 64
GATE_LORA = 128
B_LNX_EPS = 64e-5

PEER_HEADS = 8
PEER_NKEYS = 128
PEER_QDIM = 256
PEER_HALF = PEER_QDIM // 2
PEER_TOPK = 16
PEER_EXPERTS = PEER_NKEYS * PEER_NKEYS

A_WIDTH = A_HEADS * A_HEAD_DIM
B_WIDTH = B_HEADS * B_HEAD_DIM
N_BRANCH = 2
A_SIZES = (A_WIDTH, A_KV_RANK, IDX_HEADS * IDX_DIM, IDX_DIM, IDX_HEADS)
B_SIZES = (B_WIDTH, B_WIDTH, B_WIDTH, DECAY_LORA, AAA_LORA, GATE_LORA)
B_COLS = sum(B_SIZES)
GATE_COLS = N_BRANCH * D_MODEL
IN_SIZES = A_SIZES + (B_COLS, GATE_COLS)
N_IN = sum(IN_SIZES)
LN_EPS = 1e-5

kernel_name = "hybrid_dsa_rwkv7_peer_block"


def _split(t, sizes):
    idx = np.cumsum(np.array(sizes))[:-1]
    return jnp.split(t, [int(i) for i in idx], axis=-1)


def _layernorm(x, g, b, eps=LN_EPS):
    xf = x.astype(jnp.float32)
    mu = jnp.mean(xf, axis=-1, keepdims=True)
    var = jnp.mean(jnp.square(xf - mu), axis=-1, keepdims=True)
    return ((xf - mu) * lax.rsqrt(var + eps) * g + b).astype(x.dtype)


def _rmsnorm(x, g, eps=1e-6):
    xf = x.astype(jnp.float32)
    return (xf * lax.rsqrt(jnp.mean(xf * xf, axis=-1, keepdims=True) + eps) * g).astype(x.dtype)


def _chunk_ids(n):
    t = jnp.arange(n, dtype=jnp.int32)
    return jnp.where(t < N_META, 0, 1 + (t - N_META) // CHUNK)


def _token_shift(p, mu):
    prev = jnp.pad(p, ((0, 0), (1, 0), (0, 0)))[:, :-1]
    return p + (prev - p) * mu


def _dsa_mixer(q, c_kv, q_idx, k_idx, w_idx, kv_norm_g, w_uk, w_uv, kidx_g, kidx_b, k_sel):
    Bn, T = q.shape[0], q.shape[1]
    c_kv = _rmsnorm(c_kv, kv_norm_g)
    k_idx = _layernorm(k_idx, kidx_g, kidx_b)
    q_abs = jnp.einsum('bthd,rhd->bthr', q, w_uk)
    cid = _chunk_ids(T)
    nblk = T // Q_BLOCK
    w_scale = (IDX_HEADS ** -0.5) * (IDX_DIM ** -0.5)
    att_scale = A_HEAD_DIM ** -0.5

    def blocks(a):
        return jnp.moveaxis(a.reshape((Bn, nblk, Q_BLOCK) + a.shape[2:]), 1, 0)

    def one_block(args):
        qa, qi, wi, qc = args
        logits = jnp.einsum('bqhd,bsd->bqhs', qi, k_idx).astype(jnp.float32)
        score = jnp.einsum('bqh,bqhs->bqs', wi.astype(jnp.float32) * w_scale, jax.nn.relu(logits))
        admissible = cid[None, :] <= qc[:, None]
        score = jnp.where(admissible[None], score, -jnp.inf)
        top_s, top_i = lax.top_k(score, k_sel)
        valid = jnp.isfinite(top_s)
        c_sel = jax.vmap(lambda c, i: c[i])(c_kv, top_i)
        s = jnp.einsum('bqhr,bqkr->bqhk', qa, c_sel).astype(jnp.float32) * att_scale
        s = jnp.where(valid[:, :, None, :], s, -jnp.inf)
        p = jax.nn.softmax(s, axis=-1).astype(c_sel.dtype)
        o_lat = jnp.einsum('bqhk,bqkr->bqhr', p, c_sel)
        return jnp.einsum('bqhr,rhd->bqhd', o_lat, w_uv)

    out = lax.map(one_block, (blocks(q_abs), blocks(q_idx), blocks(w_idx), cid.reshape(nblk, Q_BLOCK)))
    return jnp.moveaxis(out, 0, 1).reshape(Bn, T, A_WIDTH)


def _rwkv7_mixer(r, k, v, wl, al, gl, w0, w2, a0, a2, g2, k_k, k_a, r_k, lnx_g, lnx_b):
    Bn, T = r.shape[0], r.shape[1]
    H, N = B_HEADS, B_HEAD_DIM
    f32 = jnp.float32
    w_log = -jax.nn.softplus(-(w0 + jnp.tanh(wl) @ w2).astype(f32)) - 0.5
    decay = jnp.exp(-jnp.exp(w_log))
    a = jax.nn.sigmoid((a0 + al @ a2).astype(f32))
    g = jax.nn.sigmoid(gl) @ g2
    heads = lambda t: t.reshape(Bn, T, H, N).astype(f32)
    kk = heads(k * k_k)
    kk = kk * lax.rsqrt(jnp.sum(kk * kk, axis=-1, keepdims=True) + 1e-12)
    k = k * (1.0 + (a - 1.0) * k_a)
    rh, kh, vh, ah, dh = heads(r), heads(k), heads(v), heads(a), heads(decay)
    bh = kk * ah

    def step(S, inp):
        r_t, w_t, k_t, v_t, kk_t, b_t = inp
        sa = jnp.einsum('bhvk,bhk->bhv', S, -kk_t)
        S = S * w_t[:, :, None, :] + sa[..., None] * b_t[:, :, None, :] + v_t[..., None] * k_t[:, :, None, :]
        return S, jnp.einsum('bhvk,bhk->bhv', S, r_t)

    tm = lambda t: jnp.moveaxis(t, 1, 0)
    S0 = jnp.zeros((Bn, H, N, N), f32)
    _, y = lax.scan(step, S0, (tm(rh), tm(dh), tm(kh), tm(vh), tm(kk), tm(bh)))
    y = jnp.moveaxis(y, 0, 1)
    mu = jnp.mean(y, axis=-1, keepdims=True)
    var = jnp.mean(jnp.square(y - mu), axis=-1, keepdims=True)
    y = ((y - mu) * lax.rsqrt(var + B_LNX_EPS)).reshape(Bn, T, B_WIDTH) * lnx_g + lnx_b
    bonus = jnp.sum(rh * kh * r_k.reshape(H, N).astype(f32), axis=-1, keepdims=True) * vh
    y = (y + bonus.reshape(Bn, T, B_WIDTH)) * g
    return y.astype(r.dtype)


def _peer(x, wq, subkeys, u, v):
    Bn, T, D = x.shape
    n = Bn * T
    xf = x.reshape(n, D)
    q = (xf @ wq).reshape(n, PEER_HEADS, 2, PEER_HALF)
    s = jnp.einsum('nhpd,hpkd->nhpk', q, subkeys).astype(jnp.float32)
    s_top, i_top = lax.top_k(s, PEER_TOPK)
    cand = s_top[:, :, 0, :, None] + s_top[:, :, 1, None, :]
    cand_idx = i_top[:, :, 0, :, None] * PEER_NKEYS + i_top[:, :, 1, None, :]
    best_s, best_j = lax.top_k(cand.reshape(n, PEER_HEADS, -1), PEER_TOPK)
    expert = jnp.take_along_axis(cand_idx.reshape(n, PEER_HEADS, -1), best_j, axis=-1)
    gate = jax.nn.softmax(best_s, axis=-1)
    nblk = n // Q_BLOCK

    def one_block(args):
        xb, eb, gb = args
        h = jnp.einsum('nd,nhkd->nhk', xb, u[eb]).astype(jnp.float32)
        coef = (gb * jax.nn.gelu(h, approximate=False)).astype(xb.dtype)
        return jnp.einsum('nhk,nhkd->nd', coef, v[eb])

    out = lax.map(one_block, (xf.reshape(nblk, Q_BLOCK, D),
                              expert.reshape(nblk, Q_BLOCK, PEER_HEADS, PEER_TOPK),
                              gate.reshape(nblk, Q_BLOCK, PEER_HEADS, PEER_TOPK)))
    return out.reshape(Bn, T, D)


def setup_inputs(seed: int = 0) -> dict:
    key = jax.random.key(seed)
    ks = iter(jax.random.split(key, 40))
    f32 = jnp.float32
    L, D = DEPTH, D_MODEL
    beta = (8.0 * DEPTH) ** -0.25
    nrm = lambda shape, scale: jax.random.normal(next(ks), shape, f32) * scale
    uni = lambda shape: jax.random.uniform(next(ks), shape, f32)
    return {
        "x": nrm((BATCH, SEQ, D), 1.0),
        "meta": nrm((N_META, D), 1.0),
        "ln_in_g": 1.0 + nrm((D,), 0.02),
        "ln_in_b": nrm((D,), 0.02),
        "w_in": nrm((L, D, N_IN), D ** -0.5),
        "b_gate": nrm((L, GATE_COLS), 0.02),
        "a_kv_norm_g": 1.0 + nrm((L, A_KV_RANK), 0.02),
        "a_w_uk": nrm((L, A_KV_RANK, A_HEADS, A_HEAD_DIM), A_KV_RANK ** -0.5),
        "a_w_uv": nrm((L, A_KV_RANK, A_HEADS, A_HEAD_DIM), A_KV_RANK ** -0.5),
        "a_kidx_g": 1.0 + nrm((L, IDX_DIM), 0.02),
        "a_kidx_b": nrm((L, IDX_DIM), 0.02),
        "b_mu": uni((L, B_COLS)),
        "b_w0": -6.0 + 5.0 * uni((L, B_WIDTH)),
        "b_w2": nrm((L, DECAY_LORA, B_WIDTH), 0.1),
        "b_a0": nrm((L, B_WIDTH), 0.1),
        "b_a2": nrm((L, AAA_LORA, B_WIDTH), 0.1),
        "b_g2": nrm((L, GATE_LORA, B_WIDTH), GATE_LORA ** -0.5),
        "b_k_k": 0.85 + nrm((L, B_WIDTH), 0.02),
        "b_k_a": 1.0 + nrm((L, B_WIDTH), 0.02),
        "b_r_k": nrm((L, B_WIDTH), 0.1),
        "b_lnx_g": 1.0 + nrm((L, B_WIDTH), 0.02),
        "b_lnx_b": nrm((L, B_WIDTH), 0.02),
        "w_pa": nrm((L, A_WIDTH, D), beta * A_WIDTH ** -0.5),
        "w_pb": nrm((L, B_WIDTH, D), beta * B_WIDTH ** -0.5),
        "w_o": nrm((L, D, D), beta * D ** -0.5),
        "ln1_g": 1.0 + nrm((L, D), 0.02),
        "ln1_b": nrm((L, D), 0.02),
        "peer_wq": nrm((L, D, PEER_HEADS * PEER_QDIM), D ** -0.5),
        "peer_subkeys": nrm((L, PEER_HEADS, 2, PEER_NKEYS, PEER_HALF), PEER_HALF ** -0.5),
        "peer_u": nrm((L, PEER_EXPERTS, D), D ** -0.5),
        "peer_v": nrm((L, PEER_EXPERTS, D), beta),
        "ln2_g": 1.0 + nrm((L, D), 0.02),
        "ln2_b": nrm((L, D), 0.02),
    }


def reference(x, meta, ln_in_g, ln_in_b, w_in, b_gate, a_kv_norm_g, a_w_uk, a_w_uv, a_kidx_g, a_kidx_b,
              b_mu, b_w0, b_w2, b_a0, b_a2, b_g2, b_k_k, b_k_a, b_r_k, b_lnx_g, b_lnx_b,
              w_pa, w_pb, w_o, ln1_g, ln1_b, peer_wq, peer_subkeys, peer_u, peer_v, ln2_g, ln2_b):
    Bn, S, D = x.shape
    T = N_META + S
    T_pad = -(-T // Q_BLOCK) * Q_BLOCK
    k_sel = min(TOPK_MAX, S // 4)
    alpha = (2.0 * DEPTH) ** 0.25
    h = jnp.concatenate([jnp.broadcast_to(meta[None].astype(x.dtype), (Bn, N_META, D)), x,
                         jnp.zeros((Bn, T_pad - T, D), x.dtype)], axis=1)
    h = _layernorm(h, ln_in_g, ln_in_b)
    for l in range(DEPTH):
        p = h @ w_in[l]
        aq, ackv, aqi, aki, awi, bcols, gcols = _split(p, IN_SIZES)
        bcols = _token_shift(bcols, b_mu[l])
        br, bk, bv, bwl, bal, bgl = _split(bcols, B_SIZES)
        o_a = _dsa_mixer(aq.reshape(Bn, T_pad, A_HEADS, A_HEAD_DIM), ackv,
                         aqi.reshape(Bn, T_pad, IDX_HEADS, IDX_DIM), aki, awi,
                         a_kv_norm_g[l], a_w_uk[l], a_w_uv[l], a_kidx_g[l], a_kidx_b[l], k_sel)
        o_b = _rwkv7_mixer(br, bk, bv, bwl, bal, bgl, b_w0[l], b_w2[l], b_a0[l], b_a2[l], b_g2[l],
                           b_k_k[l], b_k_a[l], b_r_k[l], b_lnx_g[l], b_lnx_b[l])
        gates = jax.nn.sigmoid((gcols + b_gate[l]).astype(jnp.float32)).astype(h.dtype)
        mixed = gates[..., :D] * (o_a @ w_pa[l]) + gates[..., D:] * (o_b @ w_pb[l])
        h = _layernorm(alpha * h + mixed @ w_o[l], ln1_g[l], ln1_b[l])
        h = _layernorm(alpha * h + _peer(h, peer_wq[l], peer_subkeys[l], peer_u[l], peer_v[l]),
                       ln2_g[l], ln2_b[l])
    return h[:, N_META:T]
```

```python
import functools
import math

import jax
import jax.numpy as jnp
from jax import lax
from jax.experimental import pallas as pl
from jax.experimental.pallas import tpu as pltpu

F32 = jnp.float32
BF16 = jnp.bfloat16
I32 = jnp.int32

D_MODEL = 1024
CHUNK = 64
N_META = 16
Q_BLOCK = 128
A_HEADS = 8
A_HEAD_DIM = 64
A_KV_RANK = 128
IDX_HEADS = 8
IDX_DIM = 64
TOPK_MAX = 256
B_HEADS = 8
B_HEAD_DIM = 64
DECAY_LORA = 64
AAA_LORA = 64
GATE_LORA = 128
B_LNX_EPS = 64e-5
PEER_HEADS = 8
PEER_NKEYS = 128
PEER_HALF = 128
PEER_TOPK = 16
A_WIDTH = A_HEADS * A_HEAD_DIM
B_WIDTH = B_HEADS * B_HEAD_DIM
B_COLS = 3 * B_WIDTH + DECAY_LORA + AAA_LORA + GATE_LORA
LN_EPS = 1e-5
DEPTH = 1

LANES = 128
VMEM_LIMIT = 52 * 1024 * 1024
PA_COLS = 512 + 128 + 128 + IDX_HEADS * LANES
INT_MIN = -2 ** 31
NEG_BIG = -1e30


def _cparams(sem):
    return pltpu.CompilerParams(dimension_semantics=sem, vmem_limit_bytes=VMEM_LIMIT)


def _layernorm(x, g, b, eps=LN_EPS):
    mu = jnp.mean(x, axis=-1, keepdims=True)
    xc = x - mu
    var = jnp.mean(xc * xc, axis=-1, keepdims=True)
    return xc * lax.rsqrt(var + eps) * g + b


def _dot(a, b):
    return jnp.dot(a, b, preferred_element_type=F32)


def _dot_nt(a, b):
    return lax.dot_general(a, b, (((1,), (1,)), ((), ())), preferred_element_type=F32)


def _dot_tn(a, b):
    return lax.dot_general(a, b, (((0,), (0,)), ((), ())), preferred_element_type=F32)


def _split2(a):
    hi = a.astype(BF16)
    lo = (a - hi.astype(F32)).astype(BF16)
    return hi, lo


def _split3(a):
    hi = a.astype(BF16)
    r = a - hi.astype(F32)
    mid = r.astype(BF16)
    lo = (r - mid.astype(F32)).astype(BF16)
    return hi, mid, lo


def _mm3(a, b, dot=_dot):
    ah, al = _split2(a)
    bh, bl = _split2(b)
    return dot(ah, bh) + (dot(ah, bl) + dot(al, bh))


def _mm_exact_rhs(a, m, dot=_dot):
    h, mid, lo = _split3(a)
    return dot(h, m) + (dot(mid, m) + dot(lo, m))


def _mm_exact_lhs(m, a, dot=_dot):
    h, mid, lo = _split3(a)
    return dot(m, h) + (dot(m, mid) + dot(m, lo))


def _ln_proj_kernel(x_ref, g_ref, b_ref, wa_ref, wb_ref, wg_ref, pa_ref, pb_ref, pg_ref):
    h = _layernorm(x_ref[...], g_ref[...], b_ref[...]).astype(BF16)
    pa_ref[...] = _dot(h, wa_ref[...])
    pb_ref[...] = _dot(h, wb_ref[...])
    pg_ref[...] = _dot(h, wg_ref[...])


def _ln_proj(x2d, g, b, wa, wb, wg, tm, interpret):
    n, d = x2d.shape
    full = lambda a: pl.BlockSpec(a.shape, lambda i: (0,) * a.ndim)
    row = lambda c: pl.BlockSpec((tm, c), lambda i: (i, 0))
    return pl.pallas_call(
        _ln_proj_kernel,
        grid=(n // tm,),
        in_specs=[row(d), full(g), full(b), full(wa), full(wb), full(wg)],
        out_specs=[row(wa.shape[1]), row(wb.shape[1]), row(wg.shape[1])],
        out_shape=[jax.ShapeDtypeStruct((n, w.shape[1]), F32) for w in (wa, wb, wg)],
        compiler_params=_cparams(("parallel",)),
        interpret=interpret,
        name="ln_proj",
    )(x2d, g, b, wa, wb, wg)


def _kv_prep_kernel(p_ref, kvg_ref, kig_ref, kib_ref, ckv_ref, kidx_ref):
    blk = p_ref[...]
    c = blk[:, :LANES]
    cn = c * lax.rsqrt(jnp.mean(c * c, axis=-1, keepdims=True) + 1e-6) * kvg_ref[...]
    ones_col = lax.broadcasted_iota(I32, c.shape, 1) == 0
    ckv_ref[...] = jnp.concatenate([cn, jnp.where(ones_col, 1.0, 0.0)], axis=1).astype(BF16)
    kx = blk[:, LANES:]
    valid = lax.broadcasted_iota(I32, kx.shape, 1) < IDX_DIM
    mu = jnp.sum(jnp.where(valid, kx, 0.0), axis=-1, keepdims=True) * (1.0 / IDX_DIM)
    dlt = jnp.where(valid, kx - mu, 0.0)
    var = jnp.sum(dlt * dlt, axis=-1, keepdims=True) * (1.0 / IDX_DIM)
    kn = dlt * lax.rsqrt(var + LN_EPS) * kig_ref[...] + kib_ref[...]
    kidx_ref[...] = jnp.where(valid, kn, 0.0).astype(BF16)


def _kv_prep(pa, kvg, kig, kib, tm, interpret):
    n = pa.shape[0]
    full = lambda a: pl.BlockSpec(a.shape, lambda i: (0,) * a.ndim)
    return pl.pallas_call(
        _kv_prep_kernel,
        grid=(n // tm,),
        in_specs=[pl.BlockSpec((tm, 2 * LANES), lambda i: (i, 2)), full(kvg), full(kig), full(kib)],
        out_specs=[pl.BlockSpec((tm, 2 * LANES), lambda i: (i, 0)),
                   pl.BlockSpec((tm, LANES), lambda i: (i, 0))],
        out_shape=[jax.ShapeDtypeStruct((n, 2 * LANES), BF16),
                   jax.ShapeDtypeStruct((n, LANES), BF16)],
        compiler_params=_cparams(("parallel",)),
        interpret=interpret,
        name="dsa_kv_prep",
    )(pa, kvg, kig, kib)


def _dsa_kernel(pa_ref, kidx_ref, ckv_ref, wuk_ref, wuv_ref, triu_ref, o_ref,
                skey_ref, wb_ref, qabs_ref, qi_ref, m_ref, acc_ref, *, k_sel, kt, w_scale, att_scale):
    qb = pl.program_id(1)
    nblk = qb + 2
    kb_per_tile = kt // LANES
    ntile = (nblk + kb_per_tile - 1) // kb_per_tile
    H = A_HEADS
    Q = Q_BLOCK

    pa = pa_ref[...]
    q = pa[:, :A_WIDTH].astype(BF16)
    for h in range(H):
        qabs_ref[h] = (_dot(q, wuk_ref[h]) * att_scale).astype(BF16)
        qi_ref[h] = pa[:, 768 + h * LANES: 768 + (h + 1) * LANES].astype(BF16)
        wcol = pa[:, 640 + IDX_DIM + h: 640 + IDX_DIM + h + 1] * w_scale
        wb_ref[h] = jnp.broadcast_to(wcol, (Q, LANES))

    row = lax.broadcasted_iota(I32, (Q, kt), 0)
    col = lax.broadcasted_iota(I32, (Q, kt), 1)
    rowlim = LANES + Q * qb + CHUNK * (row // CHUNK + 1)

    def score_tile(t, _):
        k0 = pl.multiple_of(t * kt, kt)
        kblk = kidx_ref[pl.ds(k0, kt), :]
        score = jnp.zeros((Q, kt), F32)
        for h in range(H):
            logit = _dot_nt(qi_ref[h], kblk)
            score = score + jnp.maximum(logit, 0.0) * jnp.concatenate([wb_ref[h]] * kb_per_tile, axis=1)
        score = score + 0.0
        bits = pltpu.bitcast(score, I32)
        key = jnp.where(bits < 0, bits ^ 0x7FFFFFFF, bits)
        g = col + k0
        adm = (g < N_META) | ((g >= LANES) & (g < rowlim))
        skey_ref[t] = jnp.where(adm, key, INT_MIN)
        return 0

    lax.fori_loop(0, ntile, score_tile, 0)

    def count_ge(cand):
        cand_t = jnp.concatenate([cand] * kb_per_tile, axis=1)

        def body(t, acc):
            hit = jnp.where(skey_ref[t] >= cand_t, 1.0, 0.0)
            part = hit[:, :LANES]
            for j in range(1, kb_per_tile):
                part = part + hit[:, j * LANES:(j + 1) * LANES]
            return acc + part

        acc = lax.fori_loop(0, ntile, body, jnp.zeros((Q, LANES), F32))
        return jnp.broadcast_to(jnp.sum(acc, axis=1, keepdims=True), (Q, LANES))

    def bit_body(it, thr):
        cand = thr ^ lax.shift_left(jnp.int32(1), 31 - it)
        return jnp.where(count_ge(cand) >= float(k_sel), cand, thr)

    thr = lax.fori_loop(0, 32, bit_body, jnp.full((Q, LANES), INT_MIN, I32))
    thr = jnp.maximum(thr, INT_MIN + 1)
    n_gt = count_ge(thr + 1)
    need = float(k_sel) - n_gt
    thr_t = jnp.concatenate([thr] * kb_per_tile, axis=1)

    m_ref[...] = jnp.full(m_ref.shape, NEG_BIG, F32)
    acc_ref[...] = jnp.zeros(acc_ref.shape, F32)
    ones_kt = jnp.ones((kt, LANES), BF16)

    def attn_tile(t, eq_before):
        k0 = pl.multiple_of(t * kt, kt)
        sk = skey_ref[t]
        eq = sk == thr_t
        eqf = jnp.where(eq, 1.0, 0.0).astype(BF16)
        prefix = _dot(eqf, triu_ref[...])
        pre_t = jnp.concatenate([eq_before] * kb_per_tile, axis=1)
        need_t = jnp.concatenate([need] * kb_per_tile, axis=1)
        sel = (sk > thr_t) | (eq & (prefix + pre_t < need_t))
        bias = jnp.where(sel, 0.0, NEG_BIG)
        cblk = ckv_ref[pl.ds(k0, kt), :]
        ck = cblk[:, :A_KV_RANK]
        for h in range(H):
            s = _dot_nt(qabs_ref[h], ck) + bias
            m_prev = m_ref[h]
            smax = s[:, :LANES]
            for j in range(1, kb_per_tile):
                smax = jnp.maximum(smax, s[:, j * LANES:(j + 1) * LANES])
            m_new = jnp.maximum(m_prev, jnp.broadcast_to(jnp.max(smax, axis=1, keepdims=True), (Q, LANES)))
            alpha = jnp.exp(m_prev - m_new)
            p = jnp.exp(s - jnp.concatenate([m_new] * kb_per_tile, axis=1)).astype(BF16)
            acc_ref[h] = acc_ref[h] * jnp.concatenate([alpha, alpha], axis=1) + _dot(p, cblk)
            m_ref[h] = m_new
        return eq_before + _dot(eqf, ones_kt)

    lax.fori_loop(0, ntile, attn_tile, jnp.zeros((Q, LANES), F32))

    out = jnp.zeros((Q, A_WIDTH), F32)
    for h in range(H):
        acc = acc_ref[h]
        denom = jnp.broadcast_to(acc[:, A_KV_RANK:A_KV_RANK + 1], (Q, A_KV_RANK))
        o_lat = (acc[:, :A_KV_RANK] / denom).astype(BF16)
        out = out + _dot(o_lat, wuv_ref[h])
    o_ref[...] = out.astype(BF16)


def _dsa(pa3, kidx_keys, ckv_keys, wuk_pad, wuv_pad, k_sel, kt, interpret):
    bn, s, _ = pa3.shape
    lk = kidx_keys.shape[1]
    nqb = s // Q_BLOCK
    ntile_max = lk // kt
    triu = jnp.triu(jnp.ones((kt, kt), F32), k=1).astype(BF16)
    full = lambda a: pl.BlockSpec(a.shape, lambda b, q: (0,) * a.ndim)
    kern = functools.partial(_dsa_kernel, k_sel=k_sel, kt=kt,
                             w_scale=(IDX_HEADS ** -0.5) * (IDX_DIM ** -0.5), att_scale=A_HEAD_DIM ** -0.5)
    return pl.pallas_call(
        kern,
        grid=(bn, nqb),
        in_specs=[pl.BlockSpec((None, Q_BLOCK, PA_COLS), lambda b, q: (b, q, 0)),
                  pl.BlockSpec((None, lk, LANES), lambda b, q: (b, 0, 0)),
                  pl.BlockSpec((None, lk, 2 * LANES), lambda b, q: (b, 0, 0)),
                  full(wuk_pad), full(wuv_pad), full(triu)],
        out_specs=pl.BlockSpec((None, Q_BLOCK, A_WIDTH), lambda b, q: (b, q, 0)),
        out_shape=jax.ShapeDtypeStruct((bn, s, A_WIDTH), BF16),
        scratch_shapes=[pltpu.VMEM((ntile_max, Q_BLOCK, kt), I32),
                        pltpu.VMEM((IDX_HEADS, Q_BLOCK, LANES), F32),
                        pltpu.VMEM((A_HEADS, Q_BLOCK, A_KV_RANK), BF16),
                        pltpu.VMEM((IDX_HEADS, Q_BLOCK, LANES), BF16),
                        pltpu.VMEM((A_HEADS, Q_BLOCK, LANES), F32),
                        pltpu.VMEM((A_HEADS, Q_BLOCK, 2 * LANES), F32)],
        compiler_params=_cparams(("parallel", "arbitrary")),
        interpret=interpret,
        name="dsa_attention",
    )(pa3, kidx_keys, ckv_keys, wuk_pad, wuv_pad, triu)


def _rwkv_prep_kernel(pb_ref, prev0_ref, mu_ref, w0_ref, a0_ref, kk_ref, ka_ref, w2a2_ref, g2_ref, hsum_ref,
                      r_ref, lw_ref, k_ref, v_ref, an_ref, b_ref, g_ref, carry_ref):
    t = pl.program_id(1)

    @pl.when(t == 0)
    def _():
        carry_ref[...] = prev0_ref[...]

    p = pb_ref[...]
    tb = p.shape[0]
    rolled = pltpu.roll(p, 1, 0)
    first = lax.broadcasted_iota(I32, p.shape, 0) == 0
    prev = jnp.where(first, jnp.broadcast_to(carry_ref[0:1, :], p.shape), rolled)
    carry_ref[0:1, :] = p[tb - 1:tb, :]
    xs = p + (prev - p) * mu_ref[...]
    W = B_WIDTH
    r, k, v = xs[:, :W], xs[:, W:2 * W], xs[:, 2 * W:3 * W]
    lora = xs[:, 3 * W:3 * W + LANES]
    lo_lane = lax.broadcasted_iota(I32, lora.shape, 1) < DECAY_LORA
    lora = jnp.where(lo_lane, jnp.tanh(lora), lora).astype(BF16)
    wa = _dot(lora, w2a2_ref[...])
    z = w0_ref[...] + wa[:, :W]
    w_log = -jax.nn.softplus(-z) - 0.5
    a = jax.nn.sigmoid(a0_ref[...] + wa[:, W:])
    gl = xs[:, 3 * W + LANES:]
    g = _dot(jax.nn.sigmoid(gl).astype(BF16), g2_ref[...])
    kk = k * kk_ref[...]
    ss = _mm_exact_rhs(kk * kk, hsum_ref[...])
    kk = kk * lax.rsqrt(ss + 1e-12)
    r_ref[...] = r
    lw_ref[...] = -jnp.exp(w_log)
    k_ref[...] = k * (1.0 + (a - 1.0) * ka_ref[...])
    v_ref[...] = v
    an_ref[...] = -kk
    b_ref[...] = kk * a
    g_ref[...] = g


def _rwkv_prep(pb3, prev0, mu, w0, a0, k_k, k_a, w2a2, g2, hsum, tb, interpret):
    bn, t, _ = pb3.shape
    full = lambda a: pl.BlockSpec(a.shape, lambda b, i: (0,) * a.ndim)
    outspec = pl.BlockSpec((None, tb, B_WIDTH), lambda b, i: (b, i, 0))
    return pl.pallas_call(
        _rwkv_prep_kernel,
        grid=(bn, t // tb),
        in_specs=[pl.BlockSpec((None, tb, B_COLS), lambda b, i: (b, i, 0)),
                  full(prev0), full(mu), full(w0), full(a0), full(k_k), full(k_a), full(w2a2), full(g2),
                  full(hsum)],
        out_specs=[outspec] * 7,
        out_shape=[jax.ShapeDtypeStruct((bn, t, B_WIDTH), F32)] * 7,
        scratch_shapes=[pltpu.VMEM((8, B_COLS), F32)],
        compiler_params=_cparams(("parallel", "arbitrary")),
        interpret=interpret,
        name="rwkv_prep",
    )(pb3, prev0, mu, w0, a0, k_k, k_a, w2a2, g2, hsum)


def _rwkv_scan_kernel(r_ref, lw_ref, k_ref, v_ref, an_ref, b_ref, s0_ref, ltri_ref, mstrict_ref, mincl_ref,
                      eye_ref, ones_ref, y_ref, sfin_ref, state_ref, *, n_batch, c):
    ci = pl.program_id(0)
    npair = B_HEADS // 2

    @pl.when(ci == 0)
    def _():
        state_ref[...] = s0_ref[...]

    lane = lax.broadcasted_iota(I32, (c, LANES), 1)
    left = lane < B_HEAD_DIM
    mstrict = mstrict_ref[...] > 0.5
    mincl = mincl_ref[...] > 0.5
    eye = eye_ref[...]

    def blockdiag(x):
        return jnp.concatenate([jnp.where(left, x, 0.0), jnp.where(left, 0.0, x)], axis=0)

    for bi in range(n_batch):
        for pj in range(npair):
            ls = slice(pj * LANES, (pj + 1) * LANES)
            lw = lw_ref[bi, :, ls]
            cl = _mm_exact_lhs(ltri_ref[...], lw)
            p_inc = jnp.exp(cl)
            p_exc = jnp.exp(cl - lw)
            p_inv = jnp.exp(-cl)
            p_end = p_inc[c - 1:c, :]
            at2 = blockdiag(an_ref[bi, :, ls] * p_exc)
            rt2 = blockdiag(r_ref[bi, :, ls] * p_inc)
            bt = b_ref[bi, :, ls] * p_inv
            kt_ = k_ref[bi, :, ls] * p_inv
            bt2 = blockdiag(bt)
            kt2 = blockdiag(kt_)
            v2 = blockdiag(v_ref[bi, :, ls])
            ar = jnp.concatenate([at2, rt2], axis=0)
            bk = jnp.concatenate([bt2, kt2], axis=0)
            quad = _mm3(ar, bk, _dot_nt)
            a_ab = jnp.where(mstrict, quad[:2 * c, :2 * c], 0.0)
            a_ak = jnp.where(mstrict, quad[:2 * c, 2 * c:], 0.0)
            a_rb = jnp.where(mincl, quad[2 * c:, :2 * c], 0.0)
            a_rk = jnp.where(mincl, quad[2 * c:, 2 * c:], 0.0)
            tinv = eye + a_ab
            mpow = a_ab
            for _ in range(int(math.log2(c)) - 1):
                mpow = _mm3(mpow, mpow)
                tinv = tinv + _mm3(tinv, mpow)
            pcol = _mm_exact_rhs(eye * p_end, ones_ref[...])
            bk_end = jnp.concatenate([bt2 * p_end, kt2 * p_end], axis=0)

            hbd = state_ref[bi * npair + pj]
            ah = _mm3(ar, hbd)
            rhs = ah[:2 * c] + _mm3(a_ak, v2)
            u = _mm3(tinv, rhs)
            uv = jnp.concatenate([u, v2], axis=0)
            y2 = ah[2 * c:] + _mm3(jnp.concatenate([a_rb, a_rk], axis=1), uv)
            y_ref[bi, :, ls] = y2[:c] + y2[c:]
            state_ref[bi * npair + pj] = hbd * pcol + _mm3(bk_end, uv, _dot_tn)

    @pl.when(ci == pl.num_programs(0) - 1)
    def _():
        sfin_ref[...] = state_ref[...]


def _rwkv_scan(r, lw, k, v, an, b, s0, c, interpret):
    bn, t, _ = r.shape
    npair = B_HEADS // 2
    idx = jnp.arange(2 * c)
    same = (idx[:, None] // c) == (idx[None, :] // c)
    mstrict = (same & ((idx[:, None] % c) > (idx[None, :] % c))).astype(F32)
    mincl = (same & ((idx[:, None] % c) >= (idx[None, :] % c))).astype(F32)
    ltri = jnp.tril(jnp.ones((c, c), F32)).astype(BF16)
    eye = jnp.eye(2 * c, dtype=F32)
    ones = jnp.ones((LANES, LANES), BF16)
    assert 2 * c == LANES
    full = lambda a: pl.BlockSpec(a.shape, lambda i: (0,) * a.ndim)
    seq = pl.BlockSpec((bn, c, B_WIDTH), lambda i: (0, i, 0))
    kern = functools.partial(_rwkv_scan_kernel, n_batch=bn, c=c)
    return pl.pallas_call(
        kern,
        grid=(t // c,),
        in_specs=[seq] * 6 + [full(s0), full(ltri), full(mstrict), full(mincl), full(eye), full(ones)],
        out_specs=[seq, full(s0)],
        out_shape=[jax.ShapeDtypeStruct((bn, t, B_WIDTH), F32),
                   jax.ShapeDtypeStruct(s0.shape, F32)],
        scratch_shapes=[pltpu.VMEM(s0.shape, F32)],
        compiler_params=_cparams(("arbitrary",)),
        interpret=interpret,
        name="rwkv_scan",
    )(r, lw, k, v, an, b, s0, ltri, mstrict, mincl, eye, ones)


def _rwkv_post_kernel(y_ref, r_ref, k_ref, v_ref, g_ref, rk_ref, lg_ref, lb_ref, hsum_ref, o_ref):
    y = y_ref[...]
    hs = hsum_ref[...]
    inv_n = 1.0 / B_HEAD_DIM
    mu = _mm_exact_rhs(y, hs) * inv_n
    d = y - mu
    var = _mm_exact_rhs(d * d, hs) * inv_n
    yn = d * lax.rsqrt(var + B_LNX_EPS) * lg_ref[...] + lb_ref[...]
    bonus = _mm_exact_rhs(r_ref[...] * k_ref[...] * rk_ref[...], hs) * v_ref[...]
    o_ref[...] = ((yn + bonus) * g_ref[...]).astype(BF16)


def _rwkv_post(y, r, k, v, g, r_k, lnx_g, lnx_b, hsum, tm, interpret):
    n = y.shape[0]
    full = lambda a: pl.BlockSpec(a.shape, lambda i: (0,) * a.ndim)
    row = pl.BlockSpec((tm, B_WIDTH), lambda i: (i, 0))
    return pl.pallas_call(
        _rwkv_post_kernel,
        grid=(n // tm,),
        in_specs=[row] * 5 + [full(r_k), full(lnx_g), full(lnx_b), full(hsum)],
        out_specs=row,
        out_shape=jax.ShapeDtypeStruct((n, B_WIDTH), BF16),
        compiler_params=_cparams(("parallel",)),
        interpret=interpret,
        name="rwkv_post",
    )(y, r, k, v, g, r_k, lnx_g, lnx_b, hsum)


def _mix_kernel(x_ref, oa_ref, ob_ref, pg_ref, lg_ref, lb_ref, bg_ref, wpa_ref, wpb_ref, wo_ref,
                l1g_ref, l1b_ref, h_ref, *, alpha):
    h0 = _layernorm(x_ref[...], lg_ref[...], lb_ref[...])
    gates = jax.nn.sigmoid(pg_ref[...] + bg_ref[...])
    mixed = (gates[:, :D_MODEL] * _dot(oa_ref[...], wpa_ref[...])
             + gates[:, D_MODEL:] * _dot(ob_ref[...], wpb_ref[...]))
    pre = alpha * h0 + _dot(mixed.astype(BF16), wo_ref[...])
    h_ref[...] = _layernorm(pre, l1g_ref[...], l1b_ref[...])


def _mix(x2d, oa, ob, pg, lg, lb, bg, wpa, wpb, wo, l1g, l1b, alpha, tm, interpret):
    n = x2d.shape[0]
    full = lambda a: pl.BlockSpec(a.shape, lambda i: (0,) * a.ndim)
    row = lambda c: pl.BlockSpec((tm, c), lambda i: (i, 0))
    return pl.pallas_call(
        functools.partial(_mix_kernel, alpha=alpha),
        grid=(n // tm,),
        in_specs=[row(D_MODEL), row(A_WIDTH), row(B_WIDTH), row(2 * D_MODEL),
                  full(lg), full(lb), full(bg), full(wpa), full(wpb), full(wo), full(l1g), full(l1b)],
        out_specs=row(D_MODEL),
        out_shape=jax.ShapeDtypeStruct((n, D_MODEL), F32),
        compiler_params=_cparams(("parallel",)),
        interpret=interpret,
        name="mix_out_proj",
    )(x2d, oa, ob, pg, lg, lb, bg, wpa, wpb, wo, l1g, l1b)


def _top_rows(x, n):
    rows = []
    for _ in range(n):
        m = jnp.max(x, axis=0, keepdims=True)
        rows.append(m)
        x = jnp.where(x == m, -jnp.inf, x)
    return rows


def _peer_kernel(h_ref, wqt_ref, sk_ref, u_ref, vt_ref, l2g_ref, l2b_ref, o_ref,
                 hb_ref, s2_ref, e2_ref, s1_ref, e1_ref, tau_ref, acc_ref, *, alpha, eb):
    e = pl.program_id(1)
    PH = PEER_HEADS
    NK = PEER_NKEYS
    tb = h_ref.shape[0]

    @pl.when(e == 0)
    def _():
        hb = h_ref[...].astype(BF16)
        hb_ref[...] = hb
        acc_ref[...] = jnp.zeros(acc_ref.shape, F32)
        for h in range(PH):
            tops = []
            for p in range(2):
                hp = 2 * h + p
                qt = _dot_nt(wqt_ref[hp * PEER_HALF:(hp + 1) * PEER_HALF, :], hb)
                st = _dot(sk_ref[hp], qt.astype(BF16))
                (s1_ref if p == 0 else s2_ref)[h] = st
                tops.append(_top_rows(st, PEER_TOPK))
            a_top, b_top = tops
            cand = jnp.concatenate([a_top[i] + b_top[j] for i in range(PEER_TOPK) for j in range(PEER_TOPK)],
                                   axis=0)
            tau = _top_rows(cand, PEER_TOPK)[-1]
            cmax = a_top[0] + b_top[0]
            z = jnp.sum(jnp.where(cand >= tau, jnp.exp(cand - cmax), 0.0), axis=0, keepdims=True)
            tau_ref[h] = jnp.broadcast_to(tau, (8, tb))
            e1_ref[h] = jnp.exp(s1_ref[h] - a_top[0]) / z
            e2_ref[h] = jnp.exp(s2_ref[h] - b_top[0])

    ht = _dot_nt(u_ref[...], hb_ref[...])
    act = 0.5 * ht * (1.0 + lax.erf(ht * (2.0 ** -0.5)))
    n_i = eb // NK
    parts = []
    for ii in range(n_i):
        gi = jnp.zeros((NK, tb), F32)
        for h in range(PH):
            irow = e * n_i + ii
            s1row = s1_ref[h, pl.ds(irow, 1), :]
            e1row = e1_ref[h, pl.ds(irow, 1), :]
            hit = (s2_ref[h] + s1row) >= tau_ref[h, 0:1, :]
            gi = gi + jnp.where(hit, e2_ref[h] * e1row, 0.0)
        parts.append(gi)
    gate = jnp.concatenate(parts, axis=0) if n_i > 1 else parts[0]
    coef = (gate * act).astype(BF16)
    acc_ref[...] += _dot(vt_ref[...], coef)

    @pl.when(e == pl.num_programs(1) - 1)
    def _():
        pre = alpha * h_ref[...] + acc_ref[...].T
        o_ref[...] = _layernorm(pre, l2g_ref[...], l2b_ref[...])


def _peer(h1, wqt, subkeys, u_bf, vt_bf, l2g, l2b, alpha, tb, eb, interpret):
    n = h1.shape[0]
    ne = u_bf.shape[0]
    full = lambda a: pl.BlockSpec(a.shape, lambda i, e: (0,) * a.ndim)
    return pl.pallas_call(
        functools.partial(_peer_kernel, alpha=alpha, eb=eb),
        grid=(n // tb, ne // eb),
        in_specs=[pl.BlockSpec((tb, D_MODEL), lambda i, e: (i, 0)),
                  full(wqt), full(subkeys),
                  pl.BlockSpec((eb, D_MODEL), lambda i, e: (e, 0)),
                  pl.BlockSpec((D_MODEL, eb), lambda i, e: (0, e)),
                  full(l2g), full(l2b)],
        out_specs=pl.BlockSpec((tb, D_MODEL), lambda i, e: (i, 0)),
        out_shape=jax.ShapeDtypeStruct((n, D_MODEL), F32),
        scratch_shapes=[pltpu.VMEM((tb, D_MODEL), BF16),
                        pltpu.VMEM((PEER_HEADS, PEER_NKEYS, tb), F32),
                        pltpu.VMEM((PEER_HEADS, PEER_NKEYS, tb), F32),
                        pltpu.VMEM((PEER_HEADS, PEER_NKEYS, tb), F32),
                        pltpu.VMEM((PEER_HEADS, PEER_NKEYS, tb), F32),
                        pltpu.VMEM((PEER_HEADS, 8, tb), F32),
                        pltpu.VMEM((D_MODEL, tb), F32)],
        compiler_params=_cparams(("parallel", "arbitrary")),
        interpret=interpret,
        name="peer",
    )(h1, wqt, subkeys, u_bf, vt_bf, l2g, l2b)


def _row(v):
    return v.reshape(1, -1).astype(F32)


def _forward(x, meta, ln_in_g, ln_in_b, w_in, b_gate, a_kv_norm_g, a_w_uk, a_w_uv, a_kidx_g, a_kidx_b,
             b_mu, b_w0, b_w2, b_a0, b_a2, b_g2, b_k_k, b_k_a, b_r_k, b_lnx_g, b_lnx_b,
             w_pa, w_pb, w_o, ln1_g, ln1_b, peer_wq, peer_subkeys, peer_u, peer_v, ln2_g, ln2_b,
             interpret=False, tm=256, kt=512, peer_tb=512, peer_eb=512):
    bn, s, d = x.shape
    assert s % Q_BLOCK == 0 and d == D_MODEL
    n = bn * s
    k_sel = min(TOPK_MAX, s // 4)
    alpha = (2.0 * DEPTH) ** 0.25
    l = 0
    tm = min(tm, n)

    w = w_in[l]
    o = 0
    aq = w[:, o:o + A_WIDTH]; o += A_WIDTH
    ackv = w[:, o:o + A_KV_RANK]; o += A_KV_RANK
    aqi = w[:, o:o + IDX_HEADS * IDX_DIM]; o += IDX_HEADS * IDX_DIM
    aki = w[:, o:o + IDX_DIM]; o += IDX_DIM
    awi = w[:, o:o + IDX_HEADS]; o += IDX_HEADS
    wbc = w[:, o:o + B_COLS]; o += B_COLS
    wgc = w[:, o:o + 2 * D_MODEL]
    misc = jnp.concatenate([aki, awi, jnp.zeros((d, LANES - IDX_DIM - IDX_HEADS), F32)], axis=1)
    aqi_p = jnp.pad(aqi.reshape(d, IDX_HEADS, IDX_DIM), ((0, 0), (0, 0), (0, LANES - IDX_DIM)))
    wa = jnp.concatenate([aq, ackv, misc, aqi_p.reshape(d, IDX_HEADS * LANES)], axis=1).astype(BF16)
    wb = wbc.astype(BF16)
    wg = wgc.astype(BF16)

    wuk = jnp.transpose(a_w_uk[l], (1, 2, 0))
    wuk_pad = jnp.zeros((A_HEADS, A_HEADS, A_HEAD_DIM, A_KV_RANK), F32)
    wuk_pad = wuk_pad.at[jnp.arange(A_HEADS), jnp.arange(A_HEADS)].set(wuk)
    wuk_pad = wuk_pad.reshape(A_HEADS, A_WIDTH, A_KV_RANK).astype(BF16)
    wuv = jnp.transpose(a_w_uv[l], (1, 0, 2))
    wuv_pad = jnp.zeros((A_HEADS, A_KV_RANK, A_HEADS, A_HEAD_DIM), F32)
    wuv_pad = wuv_pad.at[jnp.arange(A_HEADS), :, jnp.arange(A_HEADS)].set(wuv)
    wuv_pad = wuv_pad.reshape(A_HEADS, A_KV_RANK, A_WIDTH).astype(BF16)
    kig = jnp.pad(a_kidx_g[l], (0, LANES - IDX_DIM)).reshape(1, LANES)
    kib = jnp.pad(a_kidx_b[l], (0, LANES - IDX_DIM)).reshape(1, LANES)

    w2a2 = jnp.zeros((LANES, 2 * B_WIDTH), F32)
    w2a2 = w2a2.at[:DECAY_LORA, :B_WIDTH].set(b_w2[l]).at[DECAY_LORA:, B_WIDTH:].set(b_a2[l]).astype(BF16)
    hid = jnp.arange(B_WIDTH) // B_HEAD_DIM
    hsum = (hid[:, None] == hid[None, :]).astype(BF16)

    lg, lb = _row(ln_in_g), _row(ln_in_b)

    x2d = x.reshape(n, d)
    pa, pb, pg = _ln_proj(x2d, lg, lb, wa, wb, wg, tm, interpret)
    pa_m, pb_m, _ = _ln_proj(meta.astype(F32), lg, lb, wa, wb, wg, N_META, interpret)

    kvg = _row(a_kv_norm_g[l])
    ckv, kidx = _kv_prep(pa, kvg, kig, kib, tm, interpret)
    ckv_m, kidx_m = _kv_prep(pa_m, kvg, kig, kib, N_META, interpret)
    nblk_max = s // Q_BLOCK + 1
    lk = -(-nblk_max * LANES // kt) * kt

    def keys(meta_rows, frame_rows):
        c = frame_rows.shape[-1]
        fr = frame_rows.reshape(bn, s, c)
        mt = jnp.broadcast_to(meta_rows[None], (bn, N_META, c))
        return jnp.concatenate([mt, jnp.zeros((bn, LANES - N_META, c), BF16), fr,
                                jnp.zeros((bn, lk - LANES - s, c), BF16)], axis=1)

    o_a = _dsa(pa.reshape(bn, s, PA_COLS), keys(kidx_m, kidx), keys(ckv_m, ckv), wuk_pad, wuv_pad,
               k_sel, kt, interpret)

    mu, w0, a0 = _row(b_mu[l]), _row(b_w0[l]), _row(b_a0[l])
    k_k, k_a = _row(b_k_k[l]), _row(b_k_a[l])
    g2 = b_g2[l].astype(BF16)
    c = CHUNK
    zero_prev = jnp.zeros((8, B_COLS), F32)
    mr, mlw, mk, mv, man, mb, _ = _rwkv_prep(pb_m.reshape(1, N_META, B_COLS), zero_prev, mu, w0, a0, k_k, k_a,
                                             w2a2, g2, hsum, N_META, interpret)
    padm = lambda a: jnp.pad(a, ((0, 0), (c - N_META, 0), (0, 0)))
    s_zero = jnp.zeros((B_HEADS // 2, LANES, LANES), F32)
    _, s_meta = _rwkv_scan(padm(mr), padm(mlw), padm(mk), padm(mv), padm(man), padm(mb), s_zero, c, interpret)
    prev0 = jnp.broadcast_to(pb_m[N_META - 1:N_META], (8, B_COLS))
    tb_prep = min(tm, s)
    r, lw, k, v, an, b, g = _rwkv_prep(pb.reshape(bn, s, B_COLS), prev0, mu, w0, a0, k_k, k_a, w2a2, g2, hsum,
                                       tb_prep, interpret)
    y, _ = _rwkv_scan(r, lw, k, v, an, b, jnp.concatenate([s_meta] * bn, axis=0), c, interpret)
    f2 = lambda a: a.reshape(n, B_WIDTH)
    o_b = _rwkv_post(f2(y), f2(r), f2(k), f2(v), f2(g), _row(b_r_k[l]), _row(b_lnx_g[l]), _row(b_lnx_b[l]),
                     hsum, tm, interpret)

    h1 = _mix(x2d, o_a.reshape(n, A_WIDTH), o_b, pg, lg, lb, _row(b_gate[l]),
              w_pa[l].astype(BF16), w_pb[l].astype(BF16), w_o[l].astype(BF16),
              _row(ln1_g[l]), _row(ln1_b[l]), alpha, tm, interpret)

    wqt = peer_wq[l].T.astype(BF16)
    subk = peer_subkeys[l].reshape(PEER_HEADS * 2, PEER_NKEYS, PEER_HALF).astype(BF16)
    u_bf = peer_u[l].astype(BF16)
    vt_bf = peer_v[l].T.astype(BF16)
    out = _peer(h1, wqt, subk, u_bf, vt_bf, _row(ln2_g[l]), _row(ln2_b[l]), alpha,
                min(peer_tb, n), peer_eb, interpret)
    return out.reshape(bn, s, d)


def kernel(x, meta, ln_in_g, ln_in_b, w_in, b_gate, a_kv_norm_g, a_w_uk, a_w_uv, a_kidx_g, a_kidx_b, b_mu, b_w0, b_w2, b_a0, b_a2, b_g2, b_k_k, b_k_a, b_r_k, b_lnx_g, b_lnx_b, w_pa, w_pb, w_o, ln1_g, ln1_b, peer_wq, peer_subkeys, peer_u, peer_v, ln2_g, ln2_b):
    return _forward(x, meta, ln_in_g, ln_in_b, w_in, b_gate, a_kv_norm_g, a_w_uk, a_w_uv, a_kidx_g, a_kidx_b,
                    b_mu, b_w0, b_w2, b_a0, b_a2, b_g2, b_k_k, b_k_a, b_r_k, b_lnx_g, b_lnx_b,
                    w_pa, w_pb, w_o, ln1_g, ln1_b, peer_wq, peer_subkeys, peer_u, peer_v, ln2_g, ln2_b)
```

```python
import functools
import math

import jax
import jax.numpy as jnp
from jax import lax
from jax.experimental import pallas as pl
from jax.experimental.pallas import tpu as pltpu

F32 = jnp.float32
BF16 = jnp.bfloat16
I32 = jnp.int32

D_MODEL = 1024
CHUNK = 64
N_META = 16
Q_BLOCK = 128
A_HEADS = 8
A_HEAD_DIM = 64
A_KV_RANK = 128
IDX_HEADS = 8
IDX_DIM = 64
TOPK_MAX = 256
B_HEADS = 8
B_HEAD_DIM = 64
DECAY_LORA = 64
AAA_LORA = 64
GATE_LORA = 128
B_LNX_EPS = 64e-5
PEER_HEADS = 8
PEER_NKEYS = 128
PEER_HALF = 128
PEER_TOPK = 16
A_WIDTH = A_HEADS * A_HEAD_DIM
B_WIDTH = B_HEADS * B_HEAD_DIM
B_COLS = 3 * B_WIDTH + DECAY_LORA + AAA_LORA + GATE_LORA
LN_EPS = 1e-5
DEPTH = 1

LANES = 128
VMEM_LIMIT = 52 * 1024 * 1024
PA_COLS = 512 + 128 + 128 + IDX_HEADS * LANES
INT_MIN = -2 ** 31
NEG_BIG = -1e30


def _cparams(sem):
    return pltpu.CompilerParams(dimension_semantics=sem, vmem_limit_bytes=VMEM_LIMIT)


def _layernorm(x, g, b, eps=LN_EPS):
    mu = jnp.mean(x, axis=-1, keepdims=True)
    xc = x - mu
    var = jnp.mean(xc * xc, axis=-1, keepdims=True)
    return xc * lax.rsqrt(var + eps) * g + b


def _dot(a, b):
    return jnp.dot(a, b, preferred_element_type=F32)


def _dot_nt(a, b):
    return lax.dot_general(a, b, (((1,), (1,)), ((), ())), preferred_element_type=F32)


def _dot_tn(a, b):
    return lax.dot_general(a, b, (((0,), (0,)), ((), ())), preferred_element_type=F32)


def _split2(a):
    hi = a.astype(BF16)
    lo = (a - hi.astype(F32)).astype(BF16)
    return hi, lo


def _split3(a):
    hi = a.astype(BF16)
    r = a - hi.astype(F32)
    mid = r.astype(BF16)
    lo = (r - mid.astype(F32)).astype(BF16)
    return hi, mid, lo


def _mm3(a, b, dot=_dot):
    ah, al = _split2(a)
    bh, bl = _split2(b)
    return dot(ah, bh) + (dot(ah, bl) + dot(al, bh))


def _mm_exact_rhs(a, m, dot=_dot):
    h, mid, lo = _split3(a)
    return dot(h, m) + (dot(mid, m) + dot(lo, m))


def _mm_exact_lhs(m, a, dot=_dot):
    h, mid, lo = _split3(a)
    return dot(m, h) + (dot(m, mid) + dot(m, lo))


def _ln_proj_kernel(x_ref, g_ref, b_ref, wa_ref, wb_ref, wg_ref, pa_ref, pb_ref, pg_ref):
    h = _layernorm(x_ref[...], g_ref[...], b_ref[...]).astype(BF16)
    pa_ref[...] = _dot(h, wa_ref[...])
    pb_ref[...] = _dot(h, wb_ref[...])
    pg_ref[...] = _dot(h, wg_ref[...])


def _ln_proj(x2d, g, b, wa, wb, wg, tm, interpret):
    n, d = x2d.shape
    full = lambda a: pl.BlockSpec(a.shape, lambda i: (0,) * a.ndim)
    row = lambda c: pl.BlockSpec((tm, c), lambda i: (i, 0))
    return pl.pallas_call(
        _ln_proj_kernel,
        grid=(n // tm,),
        in_specs=[row(d), full(g), full(b), full(wa), full(wb), full(wg)],
        out_specs=[row(wa.shape[1]), row(wb.shape[1]), row(wg.shape[1])],
        out_shape=[jax.ShapeDtypeStruct((n, w.shape[1]), F32) for w in (wa, wb, wg)],
        compiler_params=_cparams(("parallel",)),
        interpret=interpret,
        name="ln_proj",
    )(x2d, g, b, wa, wb, wg)


def _kv_prep_kernel(p_ref, kvg_ref, kig_ref, kib_ref, ckv_ref, kidx_ref):
    blk = p_ref[...]
    c = blk[:, :LANES]
    cn = c * lax.rsqrt(jnp.mean(c * c, axis=-1, keepdims=True) + 1e-6) * kvg_ref[...]
    ones_col = lax.broadcasted_iota(I32, c.shape, 1) == 0
    ckv_ref[...] = jnp.concatenate([cn, jnp.where(ones_col, 1.0, 0.0)], axis=1).astype(BF16)
    kx = blk[:, LANES:]
    valid = lax.broadcasted_iota(I32, kx.shape, 1) < IDX_DIM
    mu = jnp.sum(jnp.where(valid, kx, 0.0), axis=-1, keepdims=True) * (1.0 / IDX_DIM)
    dlt = jnp.where(valid, kx - mu, 0.0)
    var = jnp.sum(dlt * dlt, axis=-1, keepdims=True) * (1.0 / IDX_DIM)
    kn = dlt * lax.rsqrt(var + LN_EPS) * kig_ref[...] + kib_ref[...]
    kidx_ref[...] = jnp.where(valid, kn, 0.0).astype(BF16)


def _kv_prep(pa, kvg, kig, kib, tm, interpret):
    n = pa.shape[0]
    full = lambda a: pl.BlockSpec(a.shape, lambda i: (0,) * a.ndim)
    return pl.pallas_call(
        _kv_prep_kernel,
        grid=(n // tm,),
        in_specs=[pl.BlockSpec((tm, 2 * LANES), lambda i: (i, 2)), full(kvg), full(kig), full(kib)],
        out_specs=[pl.BlockSpec((tm, 2 * LANES), lambda i: (i, 0)),
                   pl.BlockSpec((tm, LANES), lambda i: (i, 0))],
        out_shape=[jax.ShapeDtypeStruct((n, 2 * LANES), BF16),
                   jax.ShapeDtypeStruct((n, LANES), BF16)],
        compiler_params=_cparams(("parallel",)),
        interpret=interpret,
        name="dsa_kv_prep",
    )(pa, kvg, kig, kib)


def _dsa_kernel(pa_ref, kidx_ref, ckv_ref, wuk_ref, wuv_ref, triu_ref, o_ref,
                skey_ref, wb_ref, qabs_ref, qi_ref, m_ref, acc_ref, *, k_sel, kt, w_scale, att_scale):
    qb = pl.program_id(1)
    nblk = qb + 2
    kb_per_tile = kt // LANES
    ntile = (nblk + kb_per_tile - 1) // kb_per_tile
    H = A_HEADS
    Q = Q_BLOCK

    pa = pa_ref[...]
    q = pa[:, :A_WIDTH].astype(BF16)
    for h in range(H):
        qabs_ref[h] = (_dot(q, wuk_ref[h]) * att_scale).astype(BF16)
        qi_ref[h] = pa[:, 768 + h * LANES: 768 + (h + 1) * LANES].astype(BF16)
        wcol = pa[:, 640 + IDX_DIM + h: 640 + IDX_DIM + h + 1] * w_scale
        wb_ref[h] = jnp.broadcast_to(wcol, (Q, LANES))

    row = lax.broadcasted_iota(I32, (Q, kt), 0)
    col = lax.broadcasted_iota(I32, (Q, kt), 1)
    rowlim = LANES + Q * qb + CHUNK * (row // CHUNK + 1)

    def score_tile(t, _):
        k0 = pl.multiple_of(t * kt, kt)
        kblk = kidx_ref[pl.ds(k0, kt), :]
        score = jnp.zeros((Q, kt), F32)
        for h in range(H):
            logit = _dot_nt(qi_ref[h], kblk)
            score = score + jnp.maximum(logit, 0.0) * jnp.concatenate([wb_ref[h]] * kb_per_tile, axis=1)
        score = score + 0.0
        bits = pltpu.bitcast(score, I32)
        key = jnp.where(bits < 0, bits ^ 0x7FFFFFFF, bits)
        g = col + k0
        adm = (g < N_META) | ((g >= LANES) & (g < rowlim))
        skey_ref[t] = jnp.where(adm, key, INT_MIN)
        return 0

    lax.fori_loop(0, ntile, score_tile, 0)

    def count_ge(cand):
        cand_t = jnp.concatenate([cand] * kb_per_tile, axis=1)

        def body(t, acc):
            hit = jnp.where(skey_ref[t] >= cand_t, 1.0, 0.0)
            part = hit[:, :LANES]
            for j in range(1, kb_per_tile):
                part = part + hit[:, j * LANES:(j + 1) * LANES]
            return acc + part

        acc = lax.fori_loop(0, ntile, body, jnp.zeros((Q, LANES), F32))
        return jnp.broadcast_to(jnp.sum(acc, axis=1, keepdims=True), (Q, LANES))

    def bit_body(it, thr):
        cand = thr ^ lax.shift_left(jnp.int32(1), 31 - it)
        return jnp.where(count_ge(cand) >= float(k_sel), cand, thr)

    thr = lax.fori_loop(0, 32, bit_body, jnp.full((Q, LANES), INT_MIN, I32))
    thr = jnp.maximum(thr, INT_MIN + 1)
    n_gt = count_ge(thr + 1)
    need = float(k_sel) - n_gt
    thr_t = jnp.concatenate([thr] * kb_per_tile, axis=1)

    m_ref[...] = jnp.full(m_ref.shape, NEG_BIG, F32)
    acc_ref[...] = jnp.zeros(acc_ref.shape, F32)
    ones_kt = jnp.ones((kt, LANES), BF16)

    def attn_tile(t, eq_before):
        k0 = pl.multiple_of(t * kt, kt)
        sk = skey_ref[t]
        eq = sk == thr_t
        eqf = jnp.where(eq, 1.0, 0.0).astype(BF16)
        prefix = _dot(eqf, triu_ref[...])
        pre_t = jnp.concatenate([eq_before] * kb_per_tile, axis=1)
        need_t = jnp.concatenate([need] * kb_per_tile, axis=1)
        sel = (sk > thr_t) | (eq & (prefix + pre_t < need_t))
        bias = jnp.where(sel, 0.0, NEG_BIG)
        cblk = ckv_ref[pl.ds(k0, kt), :]
        ck = cblk[:, :A_KV_RANK]
        for h in range(H):
            s = _dot_nt(qabs_ref[h], ck) + bias
            m_prev = m_ref[h]
            smax = s[:, :LANES]
            for j in range(1, kb_per_tile):
                smax = jnp.maximum(smax, s[:, j * LANES:(j + 1) * LANES])
            m_new = jnp.maximum(m_prev, jnp.broadcast_to(jnp.max(smax, axis=1, keepdims=True), (Q, LANES)))
            alpha = jnp.exp(m_prev - m_new)
            p = jnp.exp(s - jnp.concatenate([m_new] * kb_per_tile, axis=1)).astype(BF16)
            acc_ref[h] = acc_ref[h] * jnp.concatenate([alpha, alpha], axis=1) + _dot(p, cblk)
            m_ref[h] = m_new
        return eq_before + _dot(eqf, ones_kt)

    lax.fori_loop(0, ntile, attn_tile, jnp.zeros((Q, LANES), F32))

    out = jnp.zeros((Q, A_WIDTH), F32)
    for h in range(H):
        acc = acc_ref[h]
        denom = jnp.broadcast_to(acc[:, A_KV_RANK:A_KV_RANK + 1], (Q, A_KV_RANK))
        o_lat = (acc[:, :A_KV_RANK] / denom).astype(BF16)
        out = out + _dot(o_lat, wuv_ref[h])
    o_ref[...] = out.astype(BF16)


def _dsa(pa3, kidx_keys, ckv_keys, wuk_pad, wuv_pad, k_sel, kt, interpret):
    bn, s, _ = pa3.shape
    lk = kidx_keys.shape[1]
    nqb = s // Q_BLOCK
    ntile_max = lk // kt
    triu = jnp.triu(jnp.ones((kt, kt), F32), k=1).astype(BF16)
    full = lambda a: pl.BlockSpec(a.shape, lambda b, q: (0,) * a.ndim)
    kern = functools.partial(_dsa_kernel, k_sel=k_sel, kt=kt,
                             w_scale=(IDX_HEADS ** -0.5) * (IDX_DIM ** -0.5), att_scale=A_HEAD_DIM ** -0.5)
    return pl.pallas_call(
        kern,
        grid=(bn, nqb),
        in_specs=[pl.BlockSpec((None, Q_BLOCK, PA_COLS), lambda b, q: (b, q, 0)),
                  pl.BlockSpec((None, lk, LANES), lambda b, q: (b, 0, 0)),
                  pl.BlockSpec((None, lk, 2 * LANES), lambda b, q: (b, 0, 0)),
                  full(wuk_pad), full(wuv_pad), full(triu)],
        out_specs=pl.BlockSpec((None, Q_BLOCK, A_WIDTH), lambda b, q: (b, q, 0)),
        out_shape=jax.ShapeDtypeStruct((bn, s, A_WIDTH), BF16),
        scratch_shapes=[pltpu.VMEM((ntile_max, Q_BLOCK, kt), I32),
                        pltpu.VMEM((IDX_HEADS, Q_BLOCK, LANES), F32),
                        pltpu.VMEM((A_HEADS, Q_BLOCK, A_KV_RANK), BF16),
                        pltpu.VMEM((IDX_HEADS, Q_BLOCK, LANES), BF16),
                        pltpu.VMEM((A_HEADS, Q_BLOCK, LANES), F32),
                        pltpu.VMEM((A_HEADS, Q_BLOCK, 2 * LANES), F32)],
        compiler_params=_cparams(("parallel", "arbitrary")),
        interpret=interpret,
        name="dsa_attention",
    )(pa3, kidx_keys, ckv_keys, wuk_pad, wuv_pad, triu)


def _rwkv_prep_kernel(pb_ref, prev0_ref, mu_ref, w0_ref, a0_ref, kk_ref, ka_ref, w2a2_ref, g2_ref, hsum_ref,
                      r_ref, lw_ref, k_ref, v_ref, an_ref, b_ref, g_ref, carry_ref):
    t = pl.program_id(1)

    @pl.when(t == 0)
    def _():
        carry_ref[...] = prev0_ref[...]

    p = pb_ref[...]
    tb = p.shape[0]
    rolled = pltpu.roll(p, 1, 0)
    first = lax.broadcasted_iota(I32, p.shape, 0) == 0
    prev = jnp.where(first, jnp.broadcast_to(carry_ref[0:1, :], p.shape), rolled)
    carry_ref[0:1, :] = p[tb - 1:tb, :]
    xs = p + (prev - p) * mu_ref[...]
    W = B_WIDTH
    r, k, v = xs[:, :W], xs[:, W:2 * W], xs[:, 2 * W:3 * W]
    lora = xs[:, 3 * W:3 * W + LANES]
    lo_lane = lax.broadcasted_iota(I32, lora.shape, 1) < DECAY_LORA
    lora = jnp.where(lo_lane, jnp.tanh(lora), lora).astype(BF16)
    wa = _dot(lora, w2a2_ref[...])
    z = w0_ref[...] + wa[:, :W]
    w_log = -jax.nn.softplus(-z) - 0.5
    a = jax.nn.sigmoid(a0_ref[...] + wa[:, W:])
    gl = xs[:, 3 * W + LANES:]
    g = _dot(jax.nn.sigmoid(gl).astype(BF16), g2_ref[...])
    kk = k * kk_ref[...]
    ss = _mm_exact_rhs(kk * kk, hsum_ref[...])
    kk = kk * lax.rsqrt(ss + 1e-12)
    r_ref[...] = r
    lw_ref[...] = -jnp.exp(w_log)
    k_ref[...] = k * (1.0 + (a - 1.0) * ka_ref[...])
    v_ref[...] = v
    an_ref[...] = -kk
    b_ref[...] = kk * a
    g_ref[...] = g


def _rwkv_prep(pb3, prev0, mu, w0, a0, k_k, k_a, w2a2, g2, hsum, tb, interpret):
    bn, t, _ = pb3.shape
    full = lambda a: pl.BlockSpec(a.shape, lambda b, i: (0,) * a.ndim)
    outspec = pl.BlockSpec((None, tb, B_WIDTH), lambda b, i: (b, i, 0))
    return pl.pallas_call(
        _rwkv_prep_kernel,
        grid=(bn, t // tb),
        in_specs=[pl.BlockSpec((None, tb, B_COLS), lambda b, i: (b, i, 0)),
                  full(prev0), full(mu), full(w0), full(a0), full(k_k), full(k_a), full(w2a2), full(g2),
                  full(hsum)],
        out_specs=[outspec] * 7,
        out_shape=[jax.ShapeDtypeStruct((bn, t, B_WIDTH), F32)] * 7,
        scratch_shapes=[pltpu.VMEM((8, B_COLS), F32)],
        compiler_params=_cparams(("parallel", "arbitrary")),
        interpret=interpret,
        name="rwkv_prep",
    )(pb3, prev0, mu, w0, a0, k_k, k_a, w2a2, g2, hsum)


def _rwkv_scan_kernel(r_ref, lw_ref, k_ref, v_ref, an_ref, b_ref, s0_ref, ltri_ref, mstrict_ref, mincl_ref,
                      eye_ref, y_ref, sfin_ref, state_ref, *, n_batch, c):
    ci = pl.program_id(0)
    npair = B_HEADS // 2

    @pl.when(ci == 0)
    def _():
        state_ref[...] = s0_ref[...]

    lane = lax.broadcasted_iota(I32, (c, LANES), 1)
    left = lane < B_HEAD_DIM
    mstrict = mstrict_ref[...] > 0.5
    mincl = mincl_ref[...] > 0.5
    eye = eye_ref[...]

    def blockdiag(x):
        return jnp.concatenate([jnp.where(left, x, 0.0), jnp.where(left, 0.0, x)], axis=0)

    pairs = [(bi, pj) for bi in range(n_batch) for pj in range(npair)]
    at2, rt2, bte, kte, v2 = [], [], [], [], []
    a_ab, a_ak, a_rb, a_rk, p_end = [], [], [], [], []
    for bi, pj in pairs:
        ls = slice(pj * LANES, (pj + 1) * LANES)
        lw = lw_ref[bi, :, ls]
        cl = _mm_exact_lhs(ltri_ref[...], lw)
        p_inc = jnp.exp(cl)
        p_inv = jnp.exp(-cl)
        pe = p_inc[c - 1:c, :]
        a2 = blockdiag(an_ref[bi, :, ls] * jnp.exp(cl - lw))
        r2 = blockdiag(r_ref[bi, :, ls] * p_inc)
        b2 = blockdiag(b_ref[bi, :, ls] * p_inv)
        k2 = blockdiag(k_ref[bi, :, ls] * p_inv)
        quad = _mm3(jnp.concatenate([a2, r2], axis=0), jnp.concatenate([b2, k2], axis=0), _dot_nt)
        a_ab.append(jnp.where(mstrict, quad[:2 * c, :2 * c], 0.0))
        a_ak.append(jnp.where(mstrict, quad[:2 * c, 2 * c:], 0.0))
        a_rb.append(jnp.where(mincl, quad[2 * c:, :2 * c], 0.0))
        a_rk.append(jnp.where(mincl, quad[2 * c:, 2 * c:], 0.0))
        at2.append(a2)
        rt2.append(r2)
        bte.append(b2 * pe)
        kte.append(k2 * pe)
        v2.append(blockdiag(v_ref[bi, :, ls]))
        p_end.append(pe)

    tinv = [eye + m for m in a_ab]
    mpow = a_ab
    for _ in range(int(math.log2(c)) - 1):
        mpow = [_mm3(m, m) for m in mpow]
        tinv = [t + _mm3(t, m) for t, m in zip(tinv, mpow)]

    av = [_mm3(a, v) for a, v in zip(a_ak, v2)]
    tatv = [_mm3(t, jnp.concatenate([a, x], axis=1)) for t, a, x in zip(tinv, at2, av)]
    rb = [_mm3(a, x) for a, x in zip(a_rb, tatv)]
    hb = [_mm3(b, x, _dot_tn) for b, x in zip(bte, tatv)]
    rkv = [_mm3(a, v) for a, v in zip(a_rk, v2)]
    kv = [_mm3(k, v, _dot_tn) for k, v in zip(kte, v2)]

    for i, (bi, pj) in enumerate(pairs):
        ls = slice(pj * LANES, (pj + 1) * LANES)
        ra = rt2[i] + rb[i][:, :LANES]
        ha = eye * p_end[i] + hb[i][:, :LANES]
        hbd = state_ref[bi * npair + pj]
        nxt = _mm3(jnp.concatenate([ra, ha], axis=0), hbd)
        y2 = nxt[:2 * c] + rb[i][:, LANES:] + rkv[i]
        y_ref[bi, :, ls] = y2[:c] + y2[c:]
        state_ref[bi * npair + pj] = nxt[2 * c:] + hb[i][:, LANES:] + kv[i]

    @pl.when(ci == pl.num_programs(0) - 1)
    def _():
        sfin_ref[...] = state_ref[...]


def _rwkv_scan(r, lw, k, v, an, b, s0, c, interpret):
    bn, t, _ = r.shape
    npair = B_HEADS // 2
    idx = jnp.arange(2 * c)
    same = (idx[:, None] // c) == (idx[None, :] // c)
    mstrict = (same & ((idx[:, None] % c) > (idx[None, :] % c))).astype(F32)
    mincl = (same & ((idx[:, None] % c) >= (idx[None, :] % c))).astype(F32)
    ltri = jnp.tril(jnp.ones((c, c), F32)).astype(BF16)
    eye = jnp.eye(2 * c, dtype=F32)
    assert 2 * c == LANES
    full = lambda a: pl.BlockSpec(a.shape, lambda i: (0,) * a.ndim)
    seq = pl.BlockSpec((bn, c, B_WIDTH), lambda i: (0, i, 0))
    kern = functools.partial(_rwkv_scan_kernel, n_batch=bn, c=c)
    return pl.pallas_call(
        kern,
        grid=(t // c,),
        in_specs=[seq] * 6 + [full(s0), full(ltri), full(mstrict), full(mincl), full(eye)],
        out_specs=[seq, full(s0)],
        out_shape=[jax.ShapeDtypeStruct((bn, t, B_WIDTH), F32),
                   jax.ShapeDtypeStruct(s0.shape, F32)],
        scratch_shapes=[pltpu.VMEM(s0.shape, F32)],
        compiler_params=_cparams(("arbitrary",)),
        interpret=interpret,
        name="rwkv_scan",
    )(r, lw, k, v, an, b, s0, ltri, mstrict, mincl, eye)


def _rwkv_post_kernel(y_ref, r_ref, k_ref, v_ref, g_ref, rk_ref, lg_ref, lb_ref, hsum_ref, o_ref):
    y = y_ref[...]
    hs = hsum_ref[...]
    inv_n = 1.0 / B_HEAD_DIM
    mu = _mm_exact_rhs(y, hs) * inv_n
    d = y - mu
    var = _mm_exact_rhs(d * d, hs) * inv_n
    yn = d * lax.rsqrt(var + B_LNX_EPS) * lg_ref[...] + lb_ref[...]
    bonus = _mm_exact_rhs(r_ref[...] * k_ref[...] * rk_ref[...], hs) * v_ref[...]
    o_ref[...] = ((yn + bonus) * g_ref[...]).astype(BF16)


def _rwkv_post(y, r, k, v, g, r_k, lnx_g, lnx_b, hsum, tm, interpret):
    n = y.shape[0]
    full = lambda a: pl.BlockSpec(a.shape, lambda i: (0,) * a.ndim)
    row = pl.BlockSpec((tm, B_WIDTH), lambda i: (i, 0))
    return pl.pallas_call(
        _rwkv_post_kernel,
        grid=(n // tm,),
        in_specs=[row] * 5 + [full(r_k), full(lnx_g), full(lnx_b), full(hsum)],
        out_specs=row,
        out_shape=jax.ShapeDtypeStruct((n, B_WIDTH), BF16),
        compiler_params=_cparams(("parallel",)),
        interpret=interpret,
        name="rwkv_post",
    )(y, r, k, v, g, r_k, lnx_g, lnx_b, hsum)


def _mix_kernel(x_ref, oa_ref, ob_ref, pg_ref, lg_ref, lb_ref, bg_ref, wpa_ref, wpb_ref, wo_ref,
                l1g_ref, l1b_ref, h_ref, *, alpha):
    h0 = _layernorm(x_ref[...], lg_ref[...], lb_ref[...])
    gates = jax.nn.sigmoid(pg_ref[...] + bg_ref[...])
    mixed = (gates[:, :D_MODEL] * _dot(oa_ref[...], wpa_ref[...])
             + gates[:, D_MODEL:] * _dot(ob_ref[...], wpb_ref[...]))
    pre = alpha * h0 + _dot(mixed.astype(BF16), wo_ref[...])
    h_ref[...] = _layernorm(pre, l1g_ref[...], l1b_ref[...])


def _mix(x2d, oa, ob, pg, lg, lb, bg, wpa, wpb, wo, l1g, l1b, alpha, tm, interpret):
    n = x2d.shape[0]
    full = lambda a: pl.BlockSpec(a.shape, lambda i: (0,) * a.ndim)
    row = lambda c: pl.BlockSpec((tm, c), lambda i: (i, 0))
    return pl.pallas_call(
        functools.partial(_mix_kernel, alpha=alpha),
        grid=(n // tm,),
        in_specs=[row(D_MODEL), row(A_WIDTH), row(B_WIDTH), row(2 * D_MODEL),
                  full(lg), full(lb), full(bg), full(wpa), full(wpb), full(wo), full(l1g), full(l1b)],
        out_specs=row(D_MODEL),
        out_shape=jax.ShapeDtypeStruct((n, D_MODEL), F32),
        compiler_params=_cparams(("parallel",)),
        interpret=interpret,
        name="mix_out_proj",
    )(x2d, oa, ob, pg, lg, lb, bg, wpa, wpb, wo, l1g, l1b)


def _sort16_pairs():
    def merge(lo, hi, r):
        step = r * 2
        if step < hi - lo:
            yield from merge(lo, hi, step)
            yield from merge(lo + r, hi, step)
            yield from [(i, i + r) for i in range(lo + r, hi - r, step)]
        else:
            yield (lo, lo + r)

    def sort(lo, hi):
        if hi - lo >= 1:
            mid = lo + (hi - lo) // 2
            yield from sort(lo, mid)
            yield from sort(mid + 1, hi)
            yield from merge(lo, hi, 1)

    return tuple(sort(0, 15))


_SORT16 = _sort16_pairs()
SUBLANES = 8


def _ce(v, i, j):
    hi, lo = jnp.maximum(v[i], v[j]), jnp.minimum(v[i], v[j])
    v[i], v[j] = hi, lo


def _top16_sorted(st):
    v = [st[SUBLANES * k:SUBLANES * (k + 1), :] for k in range(16)]
    for i, j in _SORT16:
        _ce(v, i, j)
    for d in (4, 2, 1):
        w = [pltpu.roll(x, d, 0) for x in v]
        v = [jnp.maximum(v[k], w[15 - k]) for k in range(16)]
        for dist in (8, 4, 2, 1):
            for k in range(16):
                if not k & dist:
                    _ce(v, k, k + dist)
    return v


def _top16_ranked(st):
    x = st
    rank = jnp.full(st.shape, float(PEER_TOPK), F32)
    rows = []
    for m in range(PEER_TOPK):
        mx = jnp.max(x, axis=0, keepdims=True)
        hit = x == mx
        rank = jnp.where(hit, float(m), rank)
        x = jnp.where(hit, -jnp.inf, x)
        rows.append(mx)
    return rows, rank


def _rows_to_sublanes(rows, shape):
    sub = lax.broadcasted_iota(I32, shape, 0)
    out = jnp.broadcast_to(rows[-1], shape)
    for m in range(len(rows) - 2, -1, -1):
        out = jnp.where(sub == m, rows[m], out)
    return out


def _peer_route(s1, s2):
    tb = s1.shape[1]
    shape8 = (SUBLANES, tb)
    a = _top16_sorted(s1)
    b_rows, rank2 = _top16_ranked(s2)
    sub = lax.broadcasted_iota(I32, shape8, 0)
    b_lo = _rows_to_sublanes(b_rows[:8], shape8)
    b_hi = _rows_to_sublanes(b_rows[8:], shape8)
    a_hi = _rows_to_sublanes(a[8:], shape8)
    cands = [a[0] + b_lo, a[0] + b_hi, a[1] + b_lo]
    for i in range(2, 8):
        cands.append(jnp.where(sub < PEER_TOPK // (i + 1), a[i] + b_lo, -jnp.inf))
    cands.append(a_hi + b_rows[0])
    work = cands
    tau = None
    for r in range(PEER_TOPK):
        mx = work[0]
        for c in work[1:]:
            mx = jnp.maximum(mx, c)
        tau = jnp.max(mx, axis=0, keepdims=True)
        if r + 1 < PEER_TOPK:
            work = [jnp.where(c == tau, -jnp.inf, c) for c in work]
    cmax = a[0][0:1] + b_rows[0]
    z = jnp.zeros(shape8, F32)
    for c in cands:
        z = z + jnp.where(c >= tau, jnp.exp(c - cmax), 0.0)
    z = jnp.sum(z, axis=0, keepdims=True)
    count = jnp.zeros(s1.shape, F32)
    for m in range(PEER_TOPK):
        count = count + jnp.where((s1 + b_rows[m]) >= tau, 1.0, 0.0)
    e1 = jnp.exp(s1 - a[0][0:1]) / z
    e2 = jnp.exp(s2 - b_rows[0])
    return count, e1, rank2, e2


def _peer_kernel(h_ref, wqt_ref, sk_ref, u_ref, vt_ref, l2g_ref, l2b_ref, o_ref,
                 hb_ref, st_ref, cnt_ref, e1_ref, r2_ref, e2_ref, acc_ref, *, alpha, eb):
    e = pl.program_id(1)
    PH = PEER_HEADS
    NK = PEER_NKEYS
    tb = h_ref.shape[0]
    pack = 2 * SUBLANES

    @pl.when(e == 0)
    def _():
        hb = h_ref[...].astype(BF16)
        hb_ref[...] = hb
        acc_ref[...] = jnp.zeros(acc_ref.shape, F32)
        for h in range(PH):
            for p in range(2):
                hp = 2 * h + p
                qt = _dot_nt(wqt_ref[hp * PEER_HALF:(hp + 1) * PEER_HALF, :], hb)
                st_ref[p] = _dot(sk_ref[hp], qt.astype(BF16))

            def route_tile(lt, carry, h=h):
                sl = pl.ds(pl.multiple_of(lt * LANES, LANES), LANES)
                count, e1, rank2, e2 = _peer_route(st_ref[0, :, sl], st_ref[1, :, sl])
                cnt_ref[h, :, sl] = count
                e1_ref[h, :, sl] = e1
                r2_ref[h, :, :, sl] = rank2.astype(BF16).reshape(NK // pack, pack, LANES)
                e2_ref[h, :, :, sl] = e2.astype(BF16).reshape(NK // pack, pack, LANES)
                return carry

            lax.fori_loop(0, tb // LANES, route_tile, 0)

    hb = hb_ref[...]
    sub_e = 2 * NK
    coefs = []
    for sc in range(eb // sub_e):
        ht = _dot_nt(u_ref[sc * sub_e:(sc + 1) * sub_e, :], hb)
        act = (0.5 * ht * (1.0 + lax.erf(ht * (2.0 ** -0.5)))).astype(BF16)
        gates = []
        for ii in range(sub_e // NK):
            irow = e * (eb // NK) + sc * (sub_e // NK) + ii
            gi = jnp.zeros((NK // pack, pack, tb), BF16)
            for h in range(PH):
                cb = jnp.broadcast_to(cnt_ref[h, pl.ds(irow, 1), :], (pack, tb)).astype(BF16)
                e1b = jnp.broadcast_to(e1_ref[h, pl.ds(irow, 1), :], (pack, tb)).astype(BF16)
                hit = r2_ref[h] < cb[None]
                gi = gi + jnp.where(hit, e2_ref[h] * e1b[None], jnp.zeros((), BF16))
            gates.append(gi.reshape(NK, tb))
        coefs.append(jnp.concatenate(gates, axis=0) * act)
    coef = jnp.concatenate(coefs, axis=0) if len(coefs) > 1 else coefs[0]
    acc_ref[...] += _dot(vt_ref[...], coef)

    @pl.when(e == pl.num_programs(1) - 1)
    def _():
        pre = alpha * h_ref[...] + acc_ref[...].T
        o_ref[...] = _layernorm(pre, l2g_ref[...], l2b_ref[...])


def _peer(h1, wqt, subkeys, u_bf, vt_bf, l2g, l2b, alpha, tb, eb, interpret):
    n = h1.shape[0]
    ne = u_bf.shape[0]
    pack = 2 * SUBLANES
    full = lambda a: pl.BlockSpec(a.shape, lambda i, e: (0,) * a.ndim)
    return pl.pallas_call(
        functools.partial(_peer_kernel, alpha=alpha, eb=eb),
        grid=(n // tb, ne // eb),
        in_specs=[pl.BlockSpec((tb, D_MODEL), lambda i, e: (i, 0)),
                  full(wqt), full(subkeys),
                  pl.BlockSpec((eb, D_MODEL), lambda i, e: (e, 0)),
                  pl.BlockSpec((D_MODEL, eb), lambda i, e: (0, e)),
                  full(l2g), full(l2b)],
        out_specs=pl.BlockSpec((tb, D_MODEL), lambda i, e: (i, 0)),
        out_shape=jax.ShapeDtypeStruct((n, D_MODEL), F32),
        scratch_shapes=[pltpu.VMEM((tb, D_MODEL), BF16),
                        pltpu.VMEM((2, PEER_NKEYS, tb), F32),
                        pltpu.VMEM((PEER_HEADS, PEER_NKEYS, tb), F32),
                        pltpu.VMEM((PEER_HEADS, PEER_NKEYS, tb), F32),
                        pltpu.VMEM((PEER_HEADS, PEER_NKEYS // pack, pack, tb), BF16),
                        pltpu.VMEM((PEER_HEADS, PEER_NKEYS // pack, pack, tb), BF16),
                        pltpu.VMEM((D_MODEL, tb), F32)],
        compiler_params=_cparams(("parallel", "arbitrary")),
        interpret=interpret,
        name="peer",
    )(h1, wqt, subkeys, u_bf, vt_bf, l2g, l2b)


def _row(v):
    return v.reshape(1, -1).astype(F32)


def _forward(x, meta, ln_in_g, ln_in_b, w_in, b_gate, a_kv_norm_g, a_w_uk, a_w_uv, a_kidx_g, a_kidx_b,
             b_mu, b_w0, b_w2, b_a0, b_a2, b_g2, b_k_k, b_k_a, b_r_k, b_lnx_g, b_lnx_b,
             w_pa, w_pb, w_o, ln1_g, ln1_b, peer_wq, peer_subkeys, peer_u, peer_v, ln2_g, ln2_b,
             interpret=False, tm=256, kt=512, peer_tb=512, peer_eb=512):
    bn, s, d = x.shape
    assert s % Q_BLOCK == 0 and d == D_MODEL
    n = bn * s
    k_sel = min(TOPK_MAX, s // 4)
    alpha = (2.0 * DEPTH) ** 0.25
    l = 0
    tm = min(tm, n)

    w = w_in[l]
    o = 0
    aq = w[:, o:o + A_WIDTH]; o += A_WIDTH
    ackv = w[:, o:o + A_KV_RANK]; o += A_KV_RANK
    aqi = w[:, o:o + IDX_HEADS * IDX_DIM]; o += IDX_HEADS * IDX_DIM
    aki = w[:, o:o + IDX_DIM]; o += IDX_DIM
    awi = w[:, o:o + IDX_HEADS]; o += IDX_HEADS
    wbc = w[:, o:o + B_COLS]; o += B_COLS
    wgc = w[:, o:o + 2 * D_MODEL]
    misc = jnp.concatenate([aki, awi, jnp.zeros((d, LANES - IDX_DIM - IDX_HEADS), F32)], axis=1)
    aqi_p = jnp.pad(aqi.reshape(d, IDX_HEADS, IDX_DIM), ((0, 0), (0, 0), (0, LANES - IDX_DIM)))
    wa = jnp.concatenate([aq, ackv, misc, aqi_p.reshape(d, IDX_HEADS * LANES)], axis=1).astype(BF16)
    wb = wbc.astype(BF16)
    wg = wgc.astype(BF16)

    wuk = jnp.transpose(a_w_uk[l], (1, 2, 0))
    wuk_pad = jnp.zeros((A_HEADS, A_HEADS, A_HEAD_DIM, A_KV_RANK), F32)
    wuk_pad = wuk_pad.at[jnp.arange(A_HEADS), jnp.arange(A_HEADS)].set(wuk)
    wuk_pad = wuk_pad.reshape(A_HEADS, A_WIDTH, A_KV_RANK).astype(BF16)
    wuv = jnp.transpose(a_w_uv[l], (1, 0, 2))
    wuv_pad = jnp.zeros((A_HEADS, A_KV_RANK, A_HEADS, A_HEAD_DIM), F32)
    wuv_pad = wuv_pad.at[jnp.arange(A_HEADS), :, jnp.arange(A_HEADS)].set(wuv)
    wuv_pad = wuv_pad.reshape(A_HEADS, A_KV_RANK, A_WIDTH).astype(BF16)
    kig = jnp.pad(a_kidx_g[l], (0, LANES - IDX_DIM)).reshape(1, LANES)
    kib = jnp.pad(a_kidx_b[l], (0, LANES - IDX_DIM)).reshape(1, LANES)

    w2a2 = jnp.zeros((LANES, 2 * B_WIDTH), F32)
    w2a2 = w2a2.at[:DECAY_LORA, :B_WIDTH].set(b_w2[l]).at[DECAY_LORA:, B_WIDTH:].set(b_a2[l]).astype(BF16)
    hid = jnp.arange(B_WIDTH) // B_HEAD_DIM
    hsum = (hid[:, None] == hid[None, :]).astype(BF16)

    lg, lb = _row(ln_in_g), _row(ln_in_b)

    x2d = x.reshape(n, d)
    pa, pb, pg = _ln_proj(x2d, lg, lb, wa, wb, wg, tm, interpret)
    pa_m, pb_m, _ = _ln_proj(meta.astype(F32), lg, lb, wa, wb, wg, N_META, interpret)

    kvg = _row(a_kv_norm_g[l])
    ckv, kidx = _kv_prep(pa, kvg, kig, kib, tm, interpret)
    ckv_m, kidx_m = _kv_prep(pa_m, kvg, kig, kib, N_META, interpret)
    nblk_max = s // Q_BLOCK + 1
    lk = -(-nblk_max * LANES // kt) * kt

    def keys(meta_rows, frame_rows):
        c = frame_rows.shape[-1]
        fr = frame_rows.reshape(bn, s, c)
        mt = jnp.broadcast_to(meta_rows[None], (bn, N_META, c))
        return jnp.concatenate([mt, jnp.zeros((bn, LANES - N_META, c), BF16), fr,
                                jnp.zeros((bn, lk - LANES - s, c), BF16)], axis=1)

    o_a = _dsa(pa.reshape(bn, s, PA_COLS), keys(kidx_m, kidx), keys(ckv_m, ckv), wuk_pad, wuv_pad,
               k_sel, kt, interpret)

    mu, w0, a0 = _row(b_mu[l]), _row(b_w0[l]), _row(b_a0[l])
    k_k, k_a = _row(b_k_k[l]), _row(b_k_a[l])
    g2 = b_g2[l].astype(BF16)
    c = CHUNK
    zero_prev = jnp.zeros((8, B_COLS), F32)
    mr, mlw, mk, mv, man, mb, _ = _rwkv_prep(pb_m.reshape(1, N_META, B_COLS), zero_prev, mu, w0, a0, k_k, k_a,
                                             w2a2, g2, hsum, N_META, interpret)
    padm = lambda a: jnp.pad(a, ((0, 0), (c - N_META, 0), (0, 0)))
    s_zero = jnp.zeros((B_HEADS // 2, LANES, LANES), F32)
    _, s_meta = _rwkv_scan(padm(mr), padm(mlw), padm(mk), padm(mv), padm(man), padm(mb), s_zero, c, interpret)
    prev0 = jnp.broadcast_to(pb_m[N_META - 1:N_META], (8, B_COLS))
    tb_prep = min(tm, s)
    r, lw, k, v, an, b, g = _rwkv_prep(pb.reshape(bn, s, B_COLS), prev0, mu, w0, a0, k_k, k_a, w2a2, g2, hsum,
                                       tb_prep, interpret)
    y, _ = _rwkv_scan(r, lw, k, v, an, b, jnp.concatenate([s_meta] * bn, axis=0), c, interpret)
    f2 = lambda a: a.reshape(n, B_WIDTH)
    o_b = _rwkv_post(f2(y), f2(r), f2(k), f2(v), f2(g), _row(b_r_k[l]), _row(b_lnx_g[l]), _row(b_lnx_b[l]),
                     hsum, tm, interpret)

    h1 = _mix(x2d, o_a.reshape(n, A_WIDTH), o_b, pg, lg, lb, _row(b_gate[l]),
              w_pa[l].astype(BF16), w_pb[l].astype(BF16), w_o[l].astype(BF16),
              _row(ln1_g[l]), _row(ln1_b[l]), alpha, tm, interpret)

    wqt = peer_wq[l].T.astype(BF16)
    subk = peer_subkeys[l].reshape(PEER_HEADS * 2, PEER_NKEYS, PEER_HALF).astype(BF16)
    u_bf = peer_u[l].astype(BF16)
    vt_bf = peer_v[l].T.astype(BF16)
    out = _peer(h1, wqt, subk, u_bf, vt_bf, _row(ln2_g[l]), _row(ln2_b[l]), alpha,
                min(peer_tb, n), peer_eb, interpret)
    return out.reshape(bn, s, d)


def kernel(x, meta, ln_in_g, ln_in_b, w_in, b_gate, a_kv_norm_g, a_w_uk, a_w_uv, a_kidx_g, a_kidx_b, b_mu, b_w0, b_w2, b_a0, b_a2, b_g2, b_k_k, b_k_a, b_r_k, b_lnx_g, b_lnx_b, w_pa, w_pb, w_o, ln1_g, ln1_b, peer_wq, peer_subkeys, peer_u, peer_v, ln2_g, ln2_b):
    return _forward(x, meta, ln_in_g, ln_in_b, w_in, b_gate, a_kv_norm_g, a_w_uk, a_w_uv, a_kidx_g, a_kidx_b,
                    b_mu, b_w0, b_w2, b_a0, b_a2, b_g2, b_k_k, b_k_a, b_r_k, b_lnx_g, b_lnx_b,
                    w_pa, w_pb, w_o, ln1_g, ln1_b, peer_wq, peer_subkeys, peer_u, peer_v, ln2_g, ln2_b)
```

```python
import functools
import math

import jax
import jax.numpy as jnp
from jax import lax
from jax.experimental import pallas as pl
from jax.experimental.pallas import tpu as pltpu

F32 = jnp.float32
BF16 = jnp.bfloat16
I32 = jnp.int32

D_MODEL = 1024
CHUNK = 64
N_META = 16
Q_BLOCK = 128
A_HEADS = 8
A_HEAD_DIM = 64
A_KV_RANK = 128
IDX_HEADS = 8
IDX_DIM = 64
TOPK_MAX = 256
B_HEADS = 8
B_HEAD_DIM = 64
DECAY_LORA = 64
AAA_LORA = 64
GATE_LORA = 128
B_LNX_EPS = 64e-5
PEER_HEADS = 8
PEER_NKEYS = 128
PEER_HALF = 128
PEER_TOPK = 16
A_WIDTH = A_HEADS * A_HEAD_DIM
B_WIDTH = B_HEADS * B_HEAD_DIM
B_COLS = 3 * B_WIDTH + DECAY_LORA + AAA_LORA + GATE_LORA
LN_EPS = 1e-5
DEPTH = 1

LANES = 128
SUBLANES = 8
VMEM_LIMIT = 52 * 1024 * 1024
PA_Q = 0
PA_CKV = PA_Q + A_WIDTH
PA_MISC = PA_CKV + A_KV_RANK
PA_QI = PA_MISC + LANES
PA_COLS = PA_QI + IDX_HEADS * LANES
NEG_BIG = -1e30


def _cparams(sem):
    return pltpu.CompilerParams(dimension_semantics=sem, vmem_limit_bytes=VMEM_LIMIT)


def _layernorm(x, g, b, eps=LN_EPS):
    mu = jnp.mean(x, axis=-1, keepdims=True)
    xc = x - mu
    var = jnp.mean(xc * xc, axis=-1, keepdims=True)
    return xc * lax.rsqrt(var + eps) * g + b


def _dot(a, b):
    return jnp.dot(a, b, preferred_element_type=F32)


def _dot_nt(a, b):
    return lax.dot_general(a, b, (((1,), (1,)), ((), ())), preferred_element_type=F32)


def _dot_tn(a, b):
    return lax.dot_general(a, b, (((0,), (0,)), ((), ())), preferred_element_type=F32)


def _split2(a):
    hi = a.astype(BF16)
    lo = (a - hi.astype(F32)).astype(BF16)
    return hi, lo


def _split3(a):
    hi = a.astype(BF16)
    r = a - hi.astype(F32)
    mid = r.astype(BF16)
    lo = (r - mid.astype(F32)).astype(BF16)
    return hi, mid, lo


def _mm3(a, b, dot=_dot):
    ah, al = _split2(a)
    bh, bl = _split2(b)
    return dot(ah, bh) + (dot(ah, bl) + dot(al, bh))


def _mm_exact_rhs(a, m, dot=_dot):
    h, mid, lo = _split3(a)
    return dot(h, m) + (dot(mid, m) + dot(lo, m))


def _mm_exact_lhs(m, a, dot=_dot):
    h, mid, lo = _split3(a)
    return dot(m, h) + (dot(m, mid) + dot(m, lo))


def _ln_proj_kernel(x_ref, g_ref, b_ref, wa_ref, wb_ref, wg_ref, pa_ref, pb_ref, pg_ref):
    h = _layernorm(x_ref[...], g_ref[...], b_ref[...]).astype(BF16)
    pa_ref[...] = _dot(h, wa_ref[...])
    pb_ref[...] = _dot(h, wb_ref[...])
    pg_ref[...] = _dot(h, wg_ref[...])


def _ln_proj(x2d, g, b, wa, wb, wg, tm, interpret):
    n, d = x2d.shape
    full = lambda a: pl.BlockSpec(a.shape, lambda i: (0,) * a.ndim)
    row = lambda c: pl.BlockSpec((tm, c), lambda i: (i, 0))
    return pl.pallas_call(
        _ln_proj_kernel,
        grid=(n // tm,),
        in_specs=[row(d), full(g), full(b), full(wa), full(wb), full(wg)],
        out_specs=[row(wa.shape[1]), row(wb.shape[1]), row(wg.shape[1])],
        out_shape=[jax.ShapeDtypeStruct((n, w.shape[1]), F32) for w in (wa, wb, wg)],
        compiler_params=_cparams(("parallel",)),
        interpret=interpret,
        name="ln_proj",
    )(x2d, g, b, wa, wb, wg)


def _kv_prep_kernel(p_ref, kvg_ref, kig_ref, kib_ref, ckv_ref, kidx_ref):
    blk = p_ref[...]
    c = blk[:, :LANES]
    cn = c * lax.rsqrt(jnp.mean(c * c, axis=-1, keepdims=True) + 1e-6) * kvg_ref[...]
    ckv_ref[...] = cn.astype(BF16)
    kx = blk[:, LANES:]
    valid = lax.broadcasted_iota(I32, kx.shape, 1) < IDX_DIM
    mu = jnp.sum(jnp.where(valid, kx, 0.0), axis=-1, keepdims=True) * (1.0 / IDX_DIM)
    dlt = jnp.where(valid, kx - mu, 0.0)
    var = jnp.sum(dlt * dlt, axis=-1, keepdims=True) * (1.0 / IDX_DIM)
    kn = dlt * lax.rsqrt(var + LN_EPS) * kig_ref[...] + kib_ref[...]
    kidx_ref[...] = jnp.where(valid, kn, 0.0).astype(BF16)


def _kv_prep(pa, kvg, kig, kib, tm, interpret):
    n = pa.shape[0]
    full = lambda a: pl.BlockSpec(a.shape, lambda i: (0,) * a.ndim)
    return pl.pallas_call(
        _kv_prep_kernel,
        grid=(n // tm,),
        in_specs=[pl.BlockSpec((tm, 2 * LANES), lambda i: (i, PA_CKV // (2 * LANES))),
                  full(kvg), full(kig), full(kib)],
        out_specs=[pl.BlockSpec((tm, LANES), lambda i: (i, 0)),
                   pl.BlockSpec((tm, LANES), lambda i: (i, 0))],
        out_shape=[jax.ShapeDtypeStruct((n, LANES), BF16),
                   jax.ShapeDtypeStruct((n, LANES), BF16)],
        compiler_params=_cparams(("parallel",)),
        interpret=interpret,
        name="dsa_kv_prep",
    )(pa, kvg, kig, kib)


def _fold_rows(x, op):
    while x.shape[0] > SUBLANES:
        half = x.shape[0] // 2
        x = op(x[:half], x[half:])
    return x


def _dsa_kernel(pa_ref, kidx_ref, ckv_ref, ckvt_ref, wuk_ref, wuv_ref, tril_ref, o_ref,
                sc_ref, qabs_ref, qi_ref, wt_ref, m_ref, acc_ref, *, k_sel, kt, w_scale, att_scale):
    qb = pl.program_id(1)
    Q = Q_BLOCK
    R = A_KV_RANK
    npair = A_HEADS // 2
    kb_per_tile = kt // LANES
    nblk = qb + 2
    ntile = (nblk + kb_per_tile - 1) // kb_per_tile

    pa = pa_ref[...]
    q = pa[:, PA_Q:PA_Q + A_WIDTH].astype(BF16)
    for h in range(A_HEADS):
        rows = slice((h % 2) * Q, (h % 2 + 1) * Q)
        qabs_ref[h // 2, rows, :] = (_dot(q, wuk_ref[h]) * att_scale).astype(BF16)
        qi_ref[h // 2, rows, :] = pa[:, PA_QI + h * LANES: PA_QI + (h + 1) * LANES].astype(BF16)
    wt_ref[...] = (pa[:, PA_MISC:PA_MISC + LANES] * w_scale).T

    key_row = lax.broadcasted_iota(I32, (kt, Q), 0)
    qry = lax.broadcasted_iota(I32, (1, Q), 1)
    rowlim = LANES + Q * qb + CHUNK * (qry // CHUNK + 1)

    def score_tile(t, carry):
        hi, lo = carry
        k0 = pl.multiple_of(t * kt, kt)
        kblk = kidx_ref[pl.ds(k0, kt), :]
        logits = [_dot_nt(kblk, qi_ref[hp]) for hp in range(npair)]
        parts = []
        for hp, logit in enumerate(logits):
            logit = jnp.maximum(logit, 0.0)
            w0 = wt_ref[IDX_DIM + 2 * hp:IDX_DIM + 2 * hp + 1, :]
            w1 = wt_ref[IDX_DIM + 2 * hp + 1:IDX_DIM + 2 * hp + 2, :]
            parts.append(logit[:, :Q] * w0 + logit[:, Q:] * w1)
        score = (parts[0] + parts[1]) + (parts[2] + parts[3])
        g = key_row + k0
        adm = (g < N_META) | ((g >= LANES) & (g < rowlim))
        sc_ref[t] = jnp.where(adm, score, -jnp.inf)
        hi = jnp.maximum(hi, _fold_rows(jnp.where(adm, score, -jnp.inf), jnp.maximum))
        lo = jnp.minimum(lo, _fold_rows(jnp.where(adm, score, jnp.inf), jnp.minimum))
        return hi, lo

    hi8, lo8 = lax.fori_loop(0, ntile, score_tile, (jnp.full((SUBLANES, Q), -jnp.inf, F32),
                                                     jnp.full((SUBLANES, Q), jnp.inf, F32)))
    row_max = jnp.max(hi8, axis=0, keepdims=True)
    row_min = jnp.min(lo8, axis=0, keepdims=True)

    def count_ge(cand):
        def body(t, acc):
            hit = jnp.where(sc_ref[t] >= cand, 1.0, 0.0)
            return acc + _fold_rows(hit, jnp.add)

        acc = lax.fori_loop(0, ntile, body, jnp.zeros((SUBLANES, Q), F32))
        return jnp.sum(acc, axis=0, keepdims=True)

    def next_below(bound):
        def body(t, acc):
            sc = sc_ref[t]
            return jnp.maximum(acc, _fold_rows(jnp.where(sc < bound, sc, -jnp.inf), jnp.maximum))

        acc = lax.fori_loop(0, ntile, body, jnp.full((SUBLANES, Q), -jnp.inf, F32))
        return jnp.max(acc, axis=0, keepdims=True)

    kf = float(k_sel)
    take_all = 1e9
    n_adm = (N_META + Q * qb + CHUNK * (qry // CHUNK + 1)).astype(F32)
    c_max = count_ge(row_max)
    few = n_adm <= kf
    top_ties = jnp.logical_not(few) & (c_max >= kf)
    done = jnp.where(few | top_ties, 1.0, 0.0)
    thr = jnp.where(few, row_min, row_max)
    need = jnp.where(few, take_all, kf)
    n_eq = jnp.where(few, 0.0, c_max)

    def any_set(flag):
        return (jnp.max(jnp.where(flag, 1.0, 0.0)) > 0.0).astype(I32)

    def bisect(state):
        lo, hi, c_hi, thr, need, n_eq, done = state
        mid = lo + 0.5 * (hi - lo)
        live = (done < 0.5) & (mid > lo) & (mid < hi)
        c = count_ge(mid)
        found = live & (c == kf)
        up = live & (c > kf)
        down = live & (c < kf)
        state = (jnp.where(up, mid, lo), jnp.where(down, mid, hi), jnp.where(down, c, c_hi),
                 jnp.where(found, mid, thr), jnp.where(found, take_all, need), jnp.where(found, 0.0, n_eq),
                 jnp.where(found, 1.0, done))
        return state, up | down

    def bisect_twice(carry):
        state, _ = bisect(carry[:-1])
        state, moved = bisect(state)
        return state + (any_set(moved),)

    state = (row_min, row_max, c_max, thr, need, n_eq, done)
    state = lax.fori_loop(0, 12, lambda i, s: bisect(s)[0], state)
    state = lax.while_loop(lambda s: s[-1] > 0, bisect_twice, state + (any_set(done < 0.5),))
    _, hi, c_hi, thr, need, n_eq, done, _ = state

    def step_down(state):
        hi, c_hi, thr, need, n_eq, done, _ = state
        cand = next_below(hi)
        c = count_ge(cand)
        fin = (done < 0.5) & (c >= kf)
        go = (done < 0.5) & (c < kf)
        state = (jnp.where(go, cand, hi), jnp.where(go, c, c_hi), jnp.where(fin, cand, thr),
                 jnp.where(fin, kf - c_hi, need), jnp.where(fin, c - c_hi, n_eq), jnp.where(fin, 1.0, done))
        return state + (any_set(go),)

    state = (hi, c_hi, thr, need, n_eq, done, any_set(done < 0.5))
    _, _, thr, need, n_eq, _, _ = lax.while_loop(lambda s: s[-1] > 0, step_down, state)

    @pl.when(any_set(n_eq > need) > 0)
    def _():
        def demote(t, eq_before):
            sc = sc_ref[t]
            eq = sc == thr
            eqf = jnp.where(eq, 1.0, 0.0)
            prefix = _dot(tril_ref[...], eqf.astype(BF16))
            keep = (prefix + eq_before) < need
            sc_ref[t] = jnp.where(eq & jnp.logical_not(keep), -jnp.inf, sc)
            return eq_before + jnp.sum(eqf, axis=0, keepdims=True)

        lax.fori_loop(0, ntile, demote, jnp.zeros((1, Q), F32))

    m_ref[...] = jnp.full(m_ref.shape, NEG_BIG, F32)
    acc_ref[...] = jnp.zeros(acc_ref.shape, F32)

    def attn_tile(t, carry):
        k0 = pl.multiple_of(t * kt, kt)
        bias = jnp.where(sc_ref[t] >= thr, 0.0, NEG_BIG)
        bias2 = jnp.concatenate([bias, bias], axis=1)
        ck = ckv_ref[pl.ds(k0, kt), :]
        ckt = ckvt_ref[:, pl.ds(k0, kt)]
        ss = [_dot_nt(ck, qabs_ref[hp]) + bias2 for hp in range(npair)]
        m_prev = [m_ref[hp] for hp in range(npair)]
        m_new = [jnp.maximum(mp, jnp.max(_fold_rows(s, jnp.maximum), axis=0, keepdims=True))
                 for mp, s in zip(m_prev, ss)]
        ps = [jnp.exp(s - mn[0:1]).astype(BF16) for s, mn in zip(ss, m_new)]
        pv = [_dot(ckt, p) for p in ps]
        for hp in range(npair):
            acc_ref[hp] = acc_ref[hp] * jnp.exp(m_prev[hp][0:1] - m_new[hp][0:1]) + pv[hp]
            m_ref[hp] = m_new[hp]
        return carry

    lax.fori_loop(0, ntile, attn_tile, 0)

    out = jnp.zeros((Q, A_WIDTH), F32)
    for hp in range(npair):
        acc = acc_ref[hp]
        o_lat_t = acc[:R, :] / acc[R:R + 1, :]
        for j in range(2):
            o_lat = o_lat_t[:, j * Q:(j + 1) * Q].T.astype(BF16)
            out = out + _dot(o_lat, wuv_ref[2 * hp + j])
    o_ref[...] = out.astype(BF16)


def _dsa(pa3, kidx_keys, ckv_keys, ckvt_keys, wuk_pad, wuv_pad, k_sel, kt, interpret):
    bn, s, _ = pa3.shape
    lk = kidx_keys.shape[1]
    nqb = s // Q_BLOCK
    ntile_max = lk // kt
    npair = A_HEADS // 2
    rt = ckvt_keys.shape[1]
    tril = jnp.tril(jnp.ones((kt, kt), F32), k=-1).astype(BF16)
    full = lambda a: pl.BlockSpec(a.shape, lambda b, q: (0,) * a.ndim)
    kern = functools.partial(_dsa_kernel, k_sel=k_sel, kt=kt,
                             w_scale=(IDX_HEADS ** -0.5) * (IDX_DIM ** -0.5), att_scale=A_HEAD_DIM ** -0.5)
    return pl.pallas_call(
        kern,
        grid=(bn, nqb),
        in_specs=[pl.BlockSpec((None, Q_BLOCK, PA_COLS), lambda b, q: (b, q, 0)),
                  pl.BlockSpec((None, lk, LANES), lambda b, q: (b, 0, 0)),
                  pl.BlockSpec((None, lk, A_KV_RANK), lambda b, q: (b, 0, 0)),
                  pl.BlockSpec((None, rt, lk), lambda b, q: (b, 0, 0)),
                  full(wuk_pad), full(wuv_pad), full(tril)],
        out_specs=pl.BlockSpec((None, Q_BLOCK, A_WIDTH), lambda b, q: (b, q, 0)),
        out_shape=jax.ShapeDtypeStruct((bn, s, A_WIDTH), BF16),
        scratch_shapes=[pltpu.VMEM((ntile_max, kt, Q_BLOCK), F32),
                        pltpu.VMEM((npair, 2 * Q_BLOCK, A_KV_RANK), BF16),
                        pltpu.VMEM((npair, 2 * Q_BLOCK, LANES), BF16),
                        pltpu.VMEM((LANES, Q_BLOCK), F32),
                        pltpu.VMEM((npair, SUBLANES, 2 * Q_BLOCK), F32),
                        pltpu.VMEM((npair, rt, 2 * Q_BLOCK), F32)],
        compiler_params=_cparams(("parallel", "arbitrary")),
        interpret=interpret,
        name="dsa_attention",
    )(pa3, kidx_keys, ckv_keys, ckvt_keys, wuk_pad, wuv_pad, tril)


def _rwkv_prep_kernel(pb_ref, prev0_ref, mu_ref, w0_ref, a0_ref, kk_ref, ka_ref, w2a2_ref, g2_ref, hsum_ref,
                      r_ref, lw_ref, k_ref, v_ref, an_ref, b_ref, g_ref, carry_ref):
    t = pl.program_id(1)

    @pl.when(t == 0)
    def _():
        carry_ref[...] = prev0_ref[...]

    p = pb_ref[...]
    tb = p.shape[0]
    rolled = pltpu.roll(p, 1, 0)
    first = lax.broadcasted_iota(I32, p.shape, 0) == 0
    prev = jnp.where(first, jnp.broadcast_to(carry_ref[0:1, :], p.shape), rolled)
    carry_ref[0:1, :] = p[tb - 1:tb, :]
    xs = p + (prev - p) * mu_ref[...]
    W = B_WIDTH
    r, k, v = xs[:, :W], xs[:, W:2 * W], xs[:, 2 * W:3 * W]
    lora = xs[:, 3 * W:3 * W + LANES]
    lo_lane = lax.broadcasted_iota(I32, lora.shape, 1) < DECAY_LORA
    lora = jnp.where(lo_lane, jnp.tanh(lora), lora).astype(BF16)
    wa = _dot(lora, w2a2_ref[...])
    z = w0_ref[...] + wa[:, :W]
    w_log = -jax.nn.softplus(-z) - 0.5
    a = jax.nn.sigmoid(a0_ref[...] + wa[:, W:])
    gl = xs[:, 3 * W + LANES:]
    g = _dot(jax.nn.sigmoid(gl).astype(BF16), g2_ref[...])
    kk = k * kk_ref[...]
    ss = _mm_exact_rhs(kk * kk, hsum_ref[...])
    kk = kk * lax.rsqrt(ss + 1e-12)
    r_ref[...] = r
    lw_ref[...] = -jnp.exp(w_log)
    k_ref[...] = k * (1.0 + (a - 1.0) * ka_ref[...])
    v_ref[...] = v
    an_ref[...] = -kk
    b_ref[...] = kk * a
    g_ref[...] = g


def _rwkv_prep(pb3, prev0, mu, w0, a0, k_k, k_a, w2a2, g2, hsum, tb, interpret):
    bn, t, _ = pb3.shape
    full = lambda a: pl.BlockSpec(a.shape, lambda b, i: (0,) * a.ndim)
    outspec = pl.BlockSpec((None, tb, B_WIDTH), lambda b, i: (b, i, 0))
    return pl.pallas_call(
        _rwkv_prep_kernel,
        grid=(bn, t // tb),
        in_specs=[pl.BlockSpec((None, tb, B_COLS), lambda b, i: (b, i, 0)),
                  full(prev0), full(mu), full(w0), full(a0), full(k_k), full(k_a), full(w2a2), full(g2),
                  full(hsum)],
        out_specs=[outspec] * 7,
        out_shape=[jax.ShapeDtypeStruct((bn, t, B_WIDTH), F32)] * 7,
        scratch_shapes=[pltpu.VMEM((8, B_COLS), F32)],
        compiler_params=_cparams(("parallel", "arbitrary")),
        interpret=interpret,
        name="rwkv_prep",
    )(pb3, prev0, mu, w0, a0, k_k, k_a, w2a2, g2, hsum)


def _rwkv_scan_kernel(r_ref, lw_ref, k_ref, v_ref, an_ref, b_ref, s0_ref, ltri_ref, mstrict_ref, mincl_ref,
                      eye_ref, y_ref, sfin_ref, state_ref, *, n_batch, c):
    ci = pl.program_id(0)
    npair = B_HEADS // 2

    @pl.when(ci == 0)
    def _():
        state_ref[...] = s0_ref[...]

    lane = lax.broadcasted_iota(I32, (c, LANES), 1)
    left = lane < B_HEAD_DIM
    mstrict = mstrict_ref[...] > 0.5
    mincl = mincl_ref[...] > 0.5
    eye = eye_ref[...]

    def blockdiag(x):
        return jnp.concatenate([jnp.where(left, x, 0.0), jnp.where(left, 0.0, x)], axis=0)

    pairs = [(bi, pj) for bi in range(n_batch) for pj in range(npair)]
    at2, rt2, bte, kte, v2 = [], [], [], [], []
    a_ab, a_ak, a_rb, a_rk, p_end = [], [], [], [], []
    for bi, pj in pairs:
        ls = slice(pj * LANES, (pj + 1) * LANES)
        lw = lw_ref[bi, :, ls]
        cl = _mm_exact_lhs(ltri_ref[...], lw)
        p_inc = jnp.exp(cl)
        p_inv = jnp.exp(-cl)
        pe = p_inc[c - 1:c, :]
        a2 = blockdiag(an_ref[bi, :, ls] * jnp.exp(cl - lw))
        r2 = blockdiag(r_ref[bi, :, ls] * p_inc)
        b2 = blockdiag(b_ref[bi, :, ls] * p_inv)
        k2 = blockdiag(k_ref[bi, :, ls] * p_inv)
        quad = _mm3(jnp.concatenate([a2, r2], axis=0), jnp.concatenate([b2, k2], axis=0), _dot_nt)
        a_ab.append(jnp.where(mstrict, quad[:2 * c, :2 * c], 0.0))
        a_ak.append(jnp.where(mstrict, quad[:2 * c, 2 * c:], 0.0))
        a_rb.append(jnp.where(mincl, quad[2 * c:, :2 * c], 0.0))
        a_rk.append(jnp.where(mincl, quad[2 * c:, 2 * c:], 0.0))
        at2.append(a2)
        rt2.append(r2)
        bte.append(b2 * pe)
        kte.append(k2 * pe)
        v2.append(blockdiag(v_ref[bi, :, ls]))
        p_end.append(pe)

    tinv = [eye + m for m in a_ab]
    mpow = a_ab
    for _ in range(int(math.log2(c)) - 1):
        mpow = [_mm3(m, m) for m in mpow]
        tinv = [t + _mm3(t, m) for t, m in zip(tinv, mpow)]

    av = [_mm3(a, v) for a, v in zip(a_ak, v2)]
    tatv = [_mm3(t, jnp.concatenate([a, x], axis=1)) for t, a, x in zip(tinv, at2, av)]
    rb = [_mm3(a, x) for a, x in zip(a_rb, tatv)]
    hb = [_mm3(b, x, _dot_tn) for b, x in zip(bte, tatv)]
    rkv = [_mm3(a, v) for a, v in zip(a_rk, v2)]
    kv = [_mm3(k, v, _dot_tn) for k, v in zip(kte, v2)]

    for i, (bi, pj) in enumerate(pairs):
        ls = slice(pj * LANES, (pj + 1) * LANES)
        ra = rt2[i] + rb[i][:, :LANES]
        ha = eye * p_end[i] + hb[i][:, :LANES]
        hbd = state_ref[bi * npair + pj]
        nxt = _mm3(jnp.concatenate([ra, ha], axis=0), hbd)
        y2 = nxt[:2 * c] + rb[i][:, LANES:] + rkv[i]
        y_ref[bi, :, ls] = y2[:c] + y2[c:]
        state_ref[bi * npair + pj] = nxt[2 * c:] + hb[i][:, LANES:] + kv[i]

    @pl.when(ci == pl.num_programs(0) - 1)
    def _():
        sfin_ref[...] = state_ref[...]


def _rwkv_scan(r, lw, k, v, an, b, s0, c, interpret):
    bn, t, _ = r.shape
    npair = B_HEADS // 2
    idx = jnp.arange(2 * c)
    same = (idx[:, None] // c) == (idx[None, :] // c)
    mstrict = (same & ((idx[:, None] % c) > (idx[None, :] % c))).astype(F32)
    mincl = (same & ((idx[:, None] % c) >= (idx[None, :] % c))).astype(F32)
    ltri = jnp.tril(jnp.ones((c, c), F32)).astype(BF16)
    eye = jnp.eye(2 * c, dtype=F32)
    assert 2 * c == LANES
    full = lambda a: pl.BlockSpec(a.shape, lambda i: (0,) * a.ndim)
    seq = pl.BlockSpec((bn, c, B_WIDTH), lambda i: (0, i, 0))
    kern = functools.partial(_rwkv_scan_kernel, n_batch=bn, c=c)
    return pl.pallas_call(
        kern,
        grid=(t // c,),
        in_specs=[seq] * 6 + [full(s0), full(ltri), full(mstrict), full(mincl), full(eye)],
        out_specs=[seq, full(s0)],
        out_shape=[jax.ShapeDtypeStruct((bn, t, B_WIDTH), F32),
                   jax.ShapeDtypeStruct(s0.shape, F32)],
        scratch_shapes=[pltpu.VMEM(s0.shape, F32)],
        compiler_params=_cparams(("arbitrary",)),
        interpret=interpret,
        name="rwkv_scan",
    )(r, lw, k, v, an, b, s0, ltri, mstrict, mincl, eye)


def _rwkv_post_kernel(y_ref, r_ref, k_ref, v_ref, g_ref, rk_ref, lg_ref, lb_ref, hsum_ref, o_ref):
    y = y_ref[...]
    hs = hsum_ref[...]
    inv_n = 1.0 / B_HEAD_DIM
    mu = _mm_exact_rhs(y, hs) * inv_n
    d = y - mu
    var = _mm_exact_rhs(d * d, hs) * inv_n
    yn = d * lax.rsqrt(var + B_LNX_EPS) * lg_ref[...] + lb_ref[...]
    bonus = _mm_exact_rhs(r_ref[...] * k_ref[...] * rk_ref[...], hs) * v_ref[...]
    o_ref[...] = ((yn + bonus) * g_ref[...]).astype(BF16)


def _rwkv_post(y, r, k, v, g, r_k, lnx_g, lnx_b, hsum, tm, interpret):
    n = y.shape[0]
    full = lambda a: pl.BlockSpec(a.shape, lambda i: (0,) * a.ndim)
    row = pl.BlockSpec((tm, B_WIDTH), lambda i: (i, 0))
    return pl.pallas_call(
        _rwkv_post_kernel,
        grid=(n // tm,),
        in_specs=[row] * 5 + [full(r_k), full(lnx_g), full(lnx_b), full(hsum)],
        out_specs=row,
        out_shape=jax.ShapeDtypeStruct((n, B_WIDTH), BF16),
        compiler_params=_cparams(("parallel",)),
        interpret=interpret,
        name="rwkv_post",
    )(y, r, k, v, g, r_k, lnx_g, lnx_b, hsum)


def _mix_kernel(x_ref, oa_ref, ob_ref, pg_ref, lg_ref, lb_ref, bg_ref, wpa_ref, wpb_ref, wo_ref,
                l1g_ref, l1b_ref, h_ref, *, alpha):
    h0 = _layernorm(x_ref[...], lg_ref[...], lb_ref[...])
    gates = jax.nn.sigmoid(pg_ref[...] + bg_ref[...])
    mixed = (gates[:, :D_MODEL] * _dot(oa_ref[...], wpa_ref[...])
             + gates[:, D_MODEL:] * _dot(ob_ref[...], wpb_ref[...]))
    pre = alpha * h0 + _dot(mixed.astype(BF16), wo_ref[...])
    h_ref[...] = _layernorm(pre, l1g_ref[...], l1b_ref[...])


def _mix(x2d, oa, ob, pg, lg, lb, bg, wpa, wpb, wo, l1g, l1b, alpha, tm, interpret):
    n = x2d.shape[0]
    full = lambda a: pl.BlockSpec(a.shape, lambda i: (0,) * a.ndim)
    row = lambda c: pl.BlockSpec((tm, c), lambda i: (i, 0))
    return pl.pallas_call(
        functools.partial(_mix_kernel, alpha=alpha),
        grid=(n // tm,),
        in_specs=[row(D_MODEL), row(A_WIDTH), row(B_WIDTH), row(2 * D_MODEL),
                  full(lg), full(lb), full(bg), full(wpa), full(wpb), full(wo), full(l1g), full(l1b)],
        out_specs=row(D_MODEL),
        out_shape=jax.ShapeDtypeStruct((n, D_MODEL), F32),
        compiler_params=_cparams(("parallel",)),
        interpret=interpret,
        name="mix_out_proj",
    )(x2d, oa, ob, pg, lg, lb, bg, wpa, wpb, wo, l1g, l1b)


def _sort16_pairs():
    def merge(lo, hi, r):
        step = r * 2
        if step < hi - lo:
            yield from merge(lo, hi, step)
            yield from merge(lo + r, hi, step)
            yield from [(i, i + r) for i in range(lo + r, hi - r, step)]
        else:
            yield (lo, lo + r)

    def sort(lo, hi):
        if hi - lo >= 1:
            mid = lo + (hi - lo) // 2
            yield from sort(lo, mid)
            yield from sort(mid + 1, hi)
            yield from merge(lo, hi, 1)

    return tuple(sort(0, 15))


_SORT16 = _sort16_pairs()


def _ce(v, i, j):
    hi, lo = jnp.maximum(v[i], v[j]), jnp.minimum(v[i], v[j])
    v[i], v[j] = hi, lo


def _top16_sorted(st):
    v = [st[SUBLANES * k:SUBLANES * (k + 1), :] for k in range(16)]
    for i, j in _SORT16:
        _ce(v, i, j)
    for d in (4, 2, 1):
        w = [pltpu.roll(x, d, 0) for x in v]
        v = [jnp.maximum(v[k], w[15 - k]) for k in range(16)]
        for dist in (8, 4, 2, 1):
            for k in range(16):
                if not k & dist:
                    _ce(v, k, k + dist)
    return v


def _top16_ranked(st):
    x = st
    rank = jnp.full(st.shape, float(PEER_TOPK), F32)
    rows = []
    for m in range(PEER_TOPK):
        mx = jnp.max(x, axis=0, keepdims=True)
        hit = x == mx
        rank = jnp.where(hit, float(m), rank)
        x = jnp.where(hit, -jnp.inf, x)
        rows.append(mx)
    return rows, rank


def _rows_to_sublanes(rows, shape):
    sub = lax.broadcasted_iota(I32, shape, 0)
    out = jnp.broadcast_to(rows[-1], shape)
    for m in range(len(rows) - 2, -1, -1):
        out = jnp.where(sub == m, rows[m], out)
    return out


def _peer_route(s1, s2):
    tb = s1.shape[1]
    shape8 = (SUBLANES, tb)
    a = _top16_sorted(s1)
    b_rows, rank2 = _top16_ranked(s2)
    sub = lax.broadcasted_iota(I32, shape8, 0)
    b_lo = _rows_to_sublanes(b_rows[:8], shape8)
    b_hi = _rows_to_sublanes(b_rows[8:], shape8)
    a_hi = _rows_to_sublanes(a[8:], shape8)
    cands = [a[0] + b_lo, a[0] + b_hi, a[1] + b_lo]
    for i in range(2, 8):
        cands.append(jnp.where(sub < PEER_TOPK // (i + 1), a[i] + b_lo, -jnp.inf))
    cands.append(a_hi + b_rows[0])
    work = cands
    tau = None
    for r in range(PEER_TOPK):
        mx = work[0]
        for c in work[1:]:
            mx = jnp.maximum(mx, c)
        tau = jnp.max(mx, axis=0, keepdims=True)
        if r + 1 < PEER_TOPK:
            work = [jnp.where(c == tau, -jnp.inf, c) for c in work]
    cmax = a[0][0:1] + b_rows[0]
    z = jnp.zeros(shape8, F32)
    for c in cands:
        z = z + jnp.where(c >= tau, jnp.exp(c - cmax), 0.0)
    z = jnp.sum(z, axis=0, keepdims=True)
    count = jnp.zeros(s1.shape, F32)
    for m in range(4):
        count = count + jnp.where((s1 + b_rows[m]) >= tau, 1.0, 0.0)
    for i in range(3):
        extra = jnp.zeros(tau.shape, F32)
        for m in range(4, PEER_TOPK // (i + 1)):
            extra = extra + jnp.where((a[i][0:1] + b_rows[m]) >= tau, 1.0, 0.0)
        count = count + jnp.where(s1 == a[i][0:1], extra, 0.0)
    e1 = jnp.exp(s1 - a[0][0:1]) / z
    e2 = jnp.exp(s2 - b_rows[0])
    return count, e1, rank2, e2


def _peer_kernel(h_ref, wqt_ref, sk_ref, u_ref, vt_ref, l2g_ref, l2b_ref, o_ref,
                 hb_ref, st_ref, cnt_ref, e1_ref, r2_ref, e2_ref, acc_ref, *, alpha, eb):
    e = pl.program_id(1)
    PH = PEER_HEADS
    NK = PEER_NKEYS
    tb = h_ref.shape[0]
    pack = 2 * SUBLANES

    @pl.when(e == 0)
    def _():
        hb = h_ref[...].astype(BF16)
        hb_ref[...] = hb
        acc_ref[...] = jnp.zeros(acc_ref.shape, F32)
        for h in range(PH):
            for p in range(2):
                hp = 2 * h + p
                qt = _dot_nt(wqt_ref[hp * PEER_HALF:(hp + 1) * PEER_HALF, :], hb)
                st_ref[p] = _dot(sk_ref[hp], qt.astype(BF16))

            def route_tile(lt, carry, h=h):
                sl = pl.ds(pl.multiple_of(lt * LANES, LANES), LANES)
                count, e1, rank2, e2 = _peer_route(st_ref[0, :, sl], st_ref[1, :, sl])
                cnt_ref[h, :, sl] = count
                e1_ref[h, :, sl] = e1
                r2_ref[h, :, :, sl] = rank2.astype(BF16).reshape(NK // pack, pack, LANES)
                e2_ref[h, :, :, sl] = e2.astype(BF16).reshape(NK // pack, pack, LANES)
                return carry

            lax.fori_loop(0, tb // LANES, route_tile, 0)

    hb = hb_ref[...]
    sub_e = 2 * NK
    coefs = []
    for sc in range(eb // sub_e):
        ht = _dot_nt(u_ref[sc * sub_e:(sc + 1) * sub_e, :], hb)
        act = (0.5 * ht * (1.0 + lax.erf(ht * (2.0 ** -0.5)))).astype(BF16)
        gates = []
        for ii in range(sub_e // NK):
            irow = e * (eb // NK) + sc * (sub_e // NK) + ii
            gi = jnp.zeros((NK // pack, pack, tb), BF16)
            for h in range(PH):
                cb = jnp.broadcast_to(cnt_ref[h, pl.ds(irow, 1), :], (pack, tb)).astype(BF16)
                e1b = jnp.broadcast_to(e1_ref[h, pl.ds(irow, 1), :], (pack, tb)).astype(BF16)
                hit = r2_ref[h] < cb[None]
                gi = gi + jnp.where(hit, e2_ref[h] * e1b[None], jnp.zeros((), BF16))
            gates.append(gi.reshape(NK, tb))
        coefs.append(jnp.concatenate(gates, axis=0) * act)
    coef = jnp.concatenate(coefs, axis=0) if len(coefs) > 1 else coefs[0]
    acc_ref[...] += _dot(vt_ref[...], coef)

    @pl.when(e == pl.num_programs(1) - 1)
    def _():
        pre = alpha * h_ref[...] + acc_ref[...].T
        o_ref[...] = _layernorm(pre, l2g_ref[...], l2b_ref[...])


def _peer(h1, wqt, subkeys, u_bf, vt_bf, l2g, l2b, alpha, tb, eb, interpret):
    n = h1.shape[0]
    ne = u_bf.shape[0]
    pack = 2 * SUBLANES
    full = lambda a: pl.BlockSpec(a.shape, lambda i, e: (0,) * a.ndim, pipeline_mode=pl.Buffered(1))
    return pl.pallas_call(
        functools.partial(_peer_kernel, alpha=alpha, eb=eb),
        grid=(n // tb, ne // eb),
        in_specs=[pl.BlockSpec((tb, D_MODEL), lambda i, e: (i, 0)),
                  full(wqt), full(subkeys),
                  pl.BlockSpec((eb, D_MODEL), lambda i, e: (e, 0)),
                  pl.BlockSpec((None, D_MODEL, eb), lambda i, e: (e, 0, 0)),
                  full(l2g), full(l2b)],
        out_specs=pl.BlockSpec((tb, D_MODEL), lambda i, e: (i, 0)),
        out_shape=jax.ShapeDtypeStruct((n, D_MODEL), F32),
        scratch_shapes=[pltpu.VMEM((tb, D_MODEL), BF16),
                        pltpu.VMEM((2, PEER_NKEYS, tb), F32),
                        pltpu.VMEM((PEER_HEADS, PEER_NKEYS, tb), F32),
                        pltpu.VMEM((PEER_HEADS, PEER_NKEYS, tb), F32),
                        pltpu.VMEM((PEER_HEADS, PEER_NKEYS // pack, pack, tb), BF16),
                        pltpu.VMEM((PEER_HEADS, PEER_NKEYS // pack, pack, tb), BF16),
                        pltpu.VMEM((D_MODEL, tb), F32)],
        compiler_params=_cparams(("parallel", "arbitrary")),
        interpret=interpret,
        name="peer",
    )(h1, wqt, subkeys, u_bf, vt_bf, l2g, l2b)


def _row(v):
    return v.reshape(1, -1).astype(F32)


def _forward(x, meta, ln_in_g, ln_in_b, w_in, b_gate, a_kv_norm_g, a_w_uk, a_w_uv, a_kidx_g, a_kidx_b,
             b_mu, b_w0, b_w2, b_a0, b_a2, b_g2, b_k_k, b_k_a, b_r_k, b_lnx_g, b_lnx_b,
             w_pa, w_pb, w_o, ln1_g, ln1_b, peer_wq, peer_subkeys, peer_u, peer_v, ln2_g, ln2_b,
             interpret=False, tm=256, kt=512, peer_tb=1024, peer_eb=512):
    bn, s, d = x.shape
    assert s % Q_BLOCK == 0 and d == D_MODEL
    n = bn * s
    k_sel = min(TOPK_MAX, s // 4)
    alpha = (2.0 * DEPTH) ** 0.25
    l = 0
    tm = min(tm, n)

    w = w_in[l]
    o = 0
    aq = w[:, o:o + A_WIDTH]; o += A_WIDTH
    ackv = w[:, o:o + A_KV_RANK]; o += A_KV_RANK
    aqi = w[:, o:o + IDX_HEADS * IDX_DIM]; o += IDX_HEADS * IDX_DIM
    aki = w[:, o:o + IDX_DIM]; o += IDX_DIM
    awi = w[:, o:o + IDX_HEADS]; o += IDX_HEADS
    wbc = w[:, o:o + B_COLS]; o += B_COLS
    wgc = w[:, o:o + 2 * D_MODEL]
    misc = jnp.concatenate([aki, awi, jnp.zeros((d, LANES - IDX_DIM - IDX_HEADS), F32)], axis=1)
    aqi_p = jnp.pad(aqi.reshape(d, IDX_HEADS, IDX_DIM), ((0, 0), (0, 0), (0, LANES - IDX_DIM)))
    wa = jnp.concatenate([aq, ackv, misc, aqi_p.reshape(d, IDX_HEADS * LANES)], axis=1).astype(BF16)
    wb = wbc.astype(BF16)
    wg = wgc.astype(BF16)

    wuk = jnp.transpose(a_w_uk[l], (1, 2, 0))
    wuk_pad = jnp.zeros((A_HEADS, A_HEADS, A_HEAD_DIM, A_KV_RANK), F32)
    wuk_pad = wuk_pad.at[jnp.arange(A_HEADS), jnp.arange(A_HEADS)].set(wuk)
    wuk_pad = wuk_pad.reshape(A_HEADS, A_WIDTH, A_KV_RANK).astype(BF16)
    wuv = jnp.transpose(a_w_uv[l], (1, 0, 2))
    wuv_pad = jnp.zeros((A_HEADS, A_KV_RANK, A_HEADS, A_HEAD_DIM), F32)
    wuv_pad = wuv_pad.at[jnp.arange(A_HEADS), :, jnp.arange(A_HEADS)].set(wuv)
    wuv_pad = wuv_pad.reshape(A_HEADS, A_KV_RANK, A_WIDTH).astype(BF16)
    kig = jnp.pad(a_kidx_g[l], (0, LANES - IDX_DIM)).reshape(1, LANES)
    kib = jnp.pad(a_kidx_b[l], (0, LANES - IDX_DIM)).reshape(1, LANES)

    w2a2 = jnp.zeros((LANES, 2 * B_WIDTH), F32)
    w2a2 = w2a2.at[:DECAY_LORA, :B_WIDTH].set(b_w2[l]).at[DECAY_LORA:, B_WIDTH:].set(b_a2[l]).astype(BF16)
    hid = jnp.arange(B_WIDTH) // B_HEAD_DIM
    hsum = (hid[:, None] == hid[None, :]).astype(BF16)

    lg, lb = _row(ln_in_g), _row(ln_in_b)

    x2d = x.reshape(n, d)
    pa, pb, pg = _ln_proj(x2d, lg, lb, wa, wb, wg, tm, interpret)
    pa_m, pb_m, _ = _ln_proj(meta.astype(F32), lg, lb, wa, wb, wg, N_META, interpret)

    kvg = _row(a_kv_norm_g[l])
    ckv, kidx = _kv_prep(pa, kvg, kig, kib, tm, interpret)
    ckv_m, kidx_m = _kv_prep(pa_m, kvg, kig, kib, N_META, interpret)
    nblk_max = s // Q_BLOCK + 1
    lk = -(-nblk_max * LANES // kt) * kt

    def keys(meta_rows, frame_rows):
        c = frame_rows.shape[-1]
        fr = frame_rows.reshape(bn, s, c)
        mt = jnp.broadcast_to(meta_rows[None], (bn, N_META, c))
        return jnp.concatenate([mt, jnp.zeros((bn, LANES - N_META, c), BF16), fr,
                                jnp.zeros((bn, lk - LANES - s, c), BF16)], axis=1)

    ckv_keys = keys(ckv_m, ckv)
    ckvt_keys = jnp.concatenate([jnp.swapaxes(ckv_keys, 1, 2), jnp.ones((bn, 1, lk), BF16),
                                 jnp.zeros((bn, 2 * SUBLANES - 1, lk), BF16)], axis=1)
    o_a = _dsa(pa.reshape(bn, s, PA_COLS), keys(kidx_m, kidx), ckv_keys, ckvt_keys, wuk_pad, wuv_pad,
               k_sel, kt, interpret)

    mu, w0, a0 = _row(b_mu[l]), _row(b_w0[l]), _row(b_a0[l])
    k_k, k_a = _row(b_k_k[l]), _row(b_k_a[l])
    g2 = b_g2[l].astype(BF16)
    c = CHUNK
    zero_prev = jnp.zeros((8, B_COLS), F32)
    mr, mlw, mk, mv, man, mb, _ = _rwkv_prep(pb_m.reshape(1, N_META, B_COLS), zero_prev, mu, w0, a0, k_k, k_a,
                                             w2a2, g2, hsum, N_META, interpret)
    padm = lambda a: jnp.pad(a, ((0, 0), (c - N_META, 0), (0, 0)))
    s_zero = jnp.zeros((B_HEADS // 2, LANES, LANES), F32)
    _, s_meta = _rwkv_scan(padm(mr), padm(mlw), padm(mk), padm(mv), padm(man), padm(mb), s_zero, c, interpret)
    prev0 = jnp.broadcast_to(pb_m[N_META - 1:N_META], (8, B_COLS))
    tb_prep = min(tm, s)
    r, lw, k, v, an, b, g = _rwkv_prep(pb.reshape(bn, s, B_COLS), prev0, mu, w0, a0, k_k, k_a, w2a2, g2, hsum,
                                       tb_prep, interpret)
    y, _ = _rwkv_scan(r, lw, k, v, an, b, jnp.concatenate([s_meta] * bn, axis=0), c, interpret)
    f2 = lambda a: a.reshape(n, B_WIDTH)
    o_b = _rwkv_post(f2(y), f2(r), f2(k), f2(v), f2(g), _row(b_r_k[l]), _row(b_lnx_g[l]), _row(b_lnx_b[l]),
                     hsum, tm, interpret)

    h1 = _mix(x2d, o_a.reshape(n, A_WIDTH), o_b, pg, lg, lb, _row(b_gate[l]),
              w_pa[l].astype(BF16), w_pb[l].astype(BF16), w_o[l].astype(BF16),
              _row(ln1_g[l]), _row(ln1_b[l]), alpha, tm, interpret)

    wqt = peer_wq[l].T.astype(BF16)
    subk = peer_subkeys[l].reshape(PEER_HEADS * 2, PEER_NKEYS, PEER_HALF).astype(BF16)
    u_bf = peer_u[l].astype(BF16)
    vt_bf = jnp.swapaxes(peer_v[l].astype(BF16).reshape(-1, peer_eb, d), 1, 2)
    out = _peer(h1, wqt, subk, u_bf, vt_bf, _row(ln2_g[l]), _row(ln2_b[l]), alpha,
                min(peer_tb, n), peer_eb, interpret)
    return out.reshape(bn, s, d)


def kernel(x, meta, ln_in_g, ln_in_b, w_in, b_gate, a_kv_norm_g, a_w_uk, a_w_uv, a_kidx_g, a_kidx_b, b_mu, b_w0, b_w2, b_a0, b_a2, b_g2, b_k_k, b_k_a, b_r_k, b_lnx_g, b_lnx_b, w_pa, w_pb, w_o, ln1_g, ln1_b, peer_wq, peer_subkeys, peer_u, peer_v, ln2_g, ln2_b):
    return _forward(x, meta, ln_in_g, ln_in_b, w_in, b_gate, a_kv_norm_g, a_w_uk, a_w_uv, a_kidx_g, a_kidx_b,
                    b_mu, b_w0, b_w2, b_a0, b_a2, b_g2, b_k_k, b_k_a, b_r_k, b_lnx_g, b_lnx_b,
                    w_pa, w_pb, w_o, ln1_g, ln1_b, peer_wq, peer_subkeys, peer_u, peer_v, ln2_g, ln2_b)
```

```python
import functools
import math

import jax
import jax.numpy as jnp
from jax import lax
from jax.experimental import pallas as pl
from jax.experimental.pallas import tpu as pltpu

F32 = jnp.float32
BF16 = jnp.bfloat16
I32 = jnp.int32

D_MODEL = 1024
CHUNK = 64
N_META = 16
Q_BLOCK = 128
A_HEADS = 8
A_HEAD_DIM = 64
A_KV_RANK = 128
IDX_HEADS = 8
IDX_DIM = 64
TOPK_MAX = 256
B_HEADS = 8
B_HEAD_DIM = 64
DECAY_LORA = 64
AAA_LORA = 64
GATE_LORA = 128
B_LNX_EPS = 64e-5
PEER_HEADS = 8
PEER_NKEYS = 128
PEER_HALF = 128
PEER_TOPK = 16
A_WIDTH = A_HEADS * A_HEAD_DIM
B_WIDTH = B_HEADS * B_HEAD_DIM
B_COLS = 3 * B_WIDTH + DECAY_LORA + AAA_LORA + GATE_LORA
LN_EPS = 1e-5
DEPTH = 1

LANES = 128
SUBLANES = 8
VMEM_LIMIT = 52 * 1024 * 1024
PA_Q = 0
PA_CKV = PA_Q + A_WIDTH
PA_MISC = PA_CKV + A_KV_RANK
PA_QI = PA_MISC + LANES
PA_COLS = PA_QI + IDX_HEADS * LANES
NEG_BIG = -1e30


def _cparams(sem):
    return pltpu.CompilerParams(dimension_semantics=sem, vmem_limit_bytes=VMEM_LIMIT)


def _layernorm(x, g, b, eps=LN_EPS):
    mu = jnp.mean(x, axis=-1, keepdims=True)
    xc = x - mu
    var = jnp.mean(xc * xc, axis=-1, keepdims=True)
    return xc * lax.rsqrt(var + eps) * g + b


def _dot(a, b):
    return jnp.dot(a, b, preferred_element_type=F32)


def _dot_nt(a, b):
    return lax.dot_general(a, b, (((1,), (1,)), ((), ())), preferred_element_type=F32)


def _dot_tn(a, b):
    return lax.dot_general(a, b, (((0,), (0,)), ((), ())), preferred_element_type=F32)


def _split2(a):
    hi = a.astype(BF16)
    lo = (a - hi.astype(F32)).astype(BF16)
    return hi, lo


def _split3(a):
    hi = a.astype(BF16)
    r = a - hi.astype(F32)
    mid = r.astype(BF16)
    lo = (r - mid.astype(F32)).astype(BF16)
    return hi, mid, lo


def _mm3(a, b, dot=_dot):
    ah, al = _split2(a)
    bh, bl = _split2(b)
    return dot(ah, bh) + (dot(ah, bl) + dot(al, bh))


def _mm1(a, b, dot=_dot):
    return dot(a.astype(BF16), b.astype(BF16))


def _mm_exact_rhs(a, m, dot=_dot):
    h, mid, lo = _split3(a)
    return dot(h, m) + (dot(mid, m) + dot(lo, m))


def _mm_exact_lhs(m, a, dot=_dot):
    h, mid, lo = _split3(a)
    return dot(m, h) + (dot(m, mid) + dot(m, lo))


def _ln_proj_kernel(x_ref, g_ref, b_ref, wa_ref, wb_ref, wg_ref, pa_ref, pb_ref, pg_ref):
    h = _layernorm(x_ref[...], g_ref[...], b_ref[...]).astype(BF16)
    pa_ref[...] = _dot(h, wa_ref[...])
    pb_ref[...] = _dot(h, wb_ref[...])
    pg_ref[...] = _dot(h, wg_ref[...])


def _ln_proj(x2d, g, b, wa, wb, wg, tm, interpret):
    n, d = x2d.shape
    full = lambda a: pl.BlockSpec(a.shape, lambda i: (0,) * a.ndim)
    row = lambda c: pl.BlockSpec((tm, c), lambda i: (i, 0))
    return pl.pallas_call(
        _ln_proj_kernel,
        grid=(n // tm,),
        in_specs=[row(d), full(g), full(b), full(wa), full(wb), full(wg)],
        out_specs=[row(wa.shape[1]), row(wb.shape[1]), row(wg.shape[1])],
        out_shape=[jax.ShapeDtypeStruct((n, w.shape[1]), F32) for w in (wa, wb, wg)],
        compiler_params=_cparams(("parallel",)),
        interpret=interpret,
        name="ln_proj",
    )(x2d, g, b, wa, wb, wg)


def _kv_prep_kernel(p_ref, kvg_ref, kig_ref, kib_ref, ckv_ref, kidx_ref):
    blk = p_ref[...]
    c = blk[:, :LANES]
    cn = c * lax.rsqrt(jnp.mean(c * c, axis=-1, keepdims=True) + 1e-6) * kvg_ref[...]
    ckv_ref[...] = cn.astype(BF16)
    kx = blk[:, LANES:]
    valid = lax.broadcasted_iota(I32, kx.shape, 1) < IDX_DIM
    mu = jnp.sum(jnp.where(valid, kx, 0.0), axis=-1, keepdims=True) * (1.0 / IDX_DIM)
    dlt = jnp.where(valid, kx - mu, 0.0)
    var = jnp.sum(dlt * dlt, axis=-1, keepdims=True) * (1.0 / IDX_DIM)
    kn = dlt * lax.rsqrt(var + LN_EPS) * kig_ref[...] + kib_ref[...]
    kidx_ref[...] = jnp.where(valid, kn, 0.0).astype(BF16)


def _kv_prep(pa, kvg, kig, kib, tm, interpret):
    n = pa.shape[0]
    full = lambda a: pl.BlockSpec(a.shape, lambda i: (0,) * a.ndim)
    return pl.pallas_call(
        _kv_prep_kernel,
        grid=(n // tm,),
        in_specs=[pl.BlockSpec((tm, 2 * LANES), lambda i: (i, PA_CKV // (2 * LANES))),
                  full(kvg), full(kig), full(kib)],
        out_specs=[pl.BlockSpec((tm, LANES), lambda i: (i, 0)),
                   pl.BlockSpec((tm, LANES), lambda i: (i, 0))],
        out_shape=[jax.ShapeDtypeStruct((n, LANES), BF16),
                   jax.ShapeDtypeStruct((n, LANES), BF16)],
        compiler_params=_cparams(("parallel",)),
        interpret=interpret,
        name="dsa_kv_prep",
    )(pa, kvg, kig, kib)


def _fold_rows(x, op):
    while x.shape[0] > SUBLANES:
        half = x.shape[0] // 2
        x = op(x[:half], x[half:])
    return x


def _dsa_kernel(pa_ref, kidx_ref, ckv_ref, ckvt_ref, wuk_ref, wuv_ref, tril_ref, o_ref,
                sc_ref, qabs_ref, qi_ref, wt_ref, m_ref, acc_ref, *, k_sel, kt, w_scale, att_scale):
    qb = pl.program_id(1)
    Q = Q_BLOCK
    R = A_KV_RANK
    npair = A_HEADS // 2
    kb_per_tile = kt // LANES
    nblk = qb + 2
    ntile = (nblk + kb_per_tile - 1) // kb_per_tile

    pa = pa_ref[...]
    q = pa[:, PA_Q:PA_Q + A_WIDTH].astype(BF16)
    for h in range(A_HEADS):
        rows = slice((h % 2) * Q, (h % 2 + 1) * Q)
        qabs_ref[h // 2, rows, :] = (_dot(q, wuk_ref[h]) * att_scale).astype(BF16)
        qi_ref[h // 2, rows, :] = pa[:, PA_QI + h * LANES: PA_QI + (h + 1) * LANES].astype(BF16)
    wt_ref[...] = (pa[:, PA_MISC:PA_MISC + LANES] * w_scale).T

    key_row = lax.broadcasted_iota(I32, (kt, Q), 0)
    qry = lax.broadcasted_iota(I32, (1, Q), 1)
    rowlim = LANES + Q * qb + CHUNK * (qry // CHUNK + 1)

    def score_tile(t, carry):
        hi, lo = carry
        k0 = pl.multiple_of(t * kt, kt)
        kblk = kidx_ref[pl.ds(k0, kt), :]
        logits = [_dot_nt(kblk, qi_ref[hp]) for hp in range(npair)]
        parts = []
        for hp, logit in enumerate(logits):
            logit = jnp.maximum(logit, 0.0)
            w0 = wt_ref[IDX_DIM + 2 * hp:IDX_DIM + 2 * hp + 1, :]
            w1 = wt_ref[IDX_DIM + 2 * hp + 1:IDX_DIM + 2 * hp + 2, :]
            parts.append(logit[:, :Q] * w0 + logit[:, Q:] * w1)
        score = (parts[0] + parts[1]) + (parts[2] + parts[3])
        g = key_row + k0
        adm = (g < N_META) | ((g >= LANES) & (g < rowlim))
        sc_ref[t] = jnp.where(adm, score, -jnp.inf)
        hi = jnp.maximum(hi, _fold_rows(jnp.where(adm, score, -jnp.inf), jnp.maximum))
        lo = jnp.minimum(lo, _fold_rows(jnp.where(adm, score, jnp.inf), jnp.minimum))
        return hi, lo

    hi8, lo8 = lax.fori_loop(0, ntile, score_tile, (jnp.full((SUBLANES, Q), -jnp.inf, F32),
                                                     jnp.full((SUBLANES, Q), jnp.inf, F32)))
    row_max = jnp.max(hi8, axis=0, keepdims=True)
    row_min = jnp.min(lo8, axis=0, keepdims=True)

    def count_ge(cand):
        def body(t, acc):
            hit = jnp.where(sc_ref[t] >= cand, 1.0, 0.0)
            return acc + _fold_rows(hit, jnp.add)

        acc = lax.fori_loop(0, ntile, body, jnp.zeros((SUBLANES, Q), F32))
        return jnp.sum(acc, axis=0, keepdims=True)

    def next_below(bound):
        def body(t, acc):
            sc = sc_ref[t]
            return jnp.maximum(acc, _fold_rows(jnp.where(sc < bound, sc, -jnp.inf), jnp.maximum))

        acc = lax.fori_loop(0, ntile, body, jnp.full((SUBLANES, Q), -jnp.inf, F32))
        return jnp.max(acc, axis=0, keepdims=True)

    kf = float(k_sel)
    take_all = 1e9
    n_adm = (N_META + Q * qb + CHUNK * (qry // CHUNK + 1)).astype(F32)
    c_max = count_ge(row_max)
    few = n_adm <= kf
    top_ties = jnp.logical_not(few) & (c_max >= kf)
    done = jnp.where(few | top_ties, 1.0, 0.0)
    thr = jnp.where(few, row_min, row_max)
    need = jnp.where(few, take_all, kf)
    n_eq = jnp.where(few, 0.0, c_max)

    def any_set(flag):
        return (jnp.max(jnp.where(flag, 1.0, 0.0)) > 0.0).astype(I32)

    def bisect(state):
        lo, hi, c_hi, thr, need, n_eq, done = state
        mid = lo + 0.5 * (hi - lo)
        live = (done < 0.5) & (mid > lo) & (mid < hi)
        c = count_ge(mid)
        found = live & (c == kf)
        up = live & (c > kf)
        down = live & (c < kf)
        state = (jnp.where(up, mid, lo), jnp.where(down, mid, hi), jnp.where(down, c, c_hi),
                 jnp.where(found, mid, thr), jnp.where(found, take_all, need), jnp.where(found, 0.0, n_eq),
                 jnp.where(found, 1.0, done))
        return state, up | down

    def bisect_twice(carry):
        state, _ = bisect(carry[:-1])
        state, moved = bisect(state)
        return state + (any_set(moved),)

    state = (row_min, row_max, c_max, thr, need, n_eq, done)
    state = lax.fori_loop(0, 12, lambda i, s: bisect(s)[0], state)
    state = lax.while_loop(lambda s: s[-1] > 0, bisect_twice, state + (any_set(done < 0.5),))
    _, hi, c_hi, thr, need, n_eq, done, _ = state

    def step_down(state):
        hi, c_hi, thr, need, n_eq, done, _ = state
        cand = next_below(hi)
        c = count_ge(cand)
        fin = (done < 0.5) & (c >= kf)
        go = (done < 0.5) & (c < kf)
        state = (jnp.where(go, cand, hi), jnp.where(go, c, c_hi), jnp.where(fin, cand, thr),
                 jnp.where(fin, kf - c_hi, need), jnp.where(fin, c - c_hi, n_eq), jnp.where(fin, 1.0, done))
        return state + (any_set(go),)

    state = (hi, c_hi, thr, need, n_eq, done, any_set(done < 0.5))
    _, _, thr, need, n_eq, _, _ = lax.while_loop(lambda s: s[-1] > 0, step_down, state)

    @pl.when(any_set(n_eq > need) > 0)
    def _():
        def demote(t, eq_before):
            sc = sc_ref[t]
            eq = sc == thr
            eqf = jnp.where(eq, 1.0, 0.0)
            prefix = _dot(tril_ref[...], eqf.astype(BF16))
            keep = (prefix + eq_before) < need
            sc_ref[t] = jnp.where(eq & jnp.logical_not(keep), -jnp.inf, sc)
            return eq_before + jnp.sum(eqf, axis=0, keepdims=True)

        lax.fori_loop(0, ntile, demote, jnp.zeros((1, Q), F32))

    m_ref[...] = jnp.full(m_ref.shape, NEG_BIG, F32)
    acc_ref[...] = jnp.zeros(acc_ref.shape, F32)

    def attn_tile(t, carry):
        k0 = pl.multiple_of(t * kt, kt)
        bias = jnp.where(sc_ref[t] >= thr, 0.0, NEG_BIG)
        bias2 = jnp.concatenate([bias, bias], axis=1)
        ck = ckv_ref[pl.ds(k0, kt), :]
        ckt = ckvt_ref[:, pl.ds(k0, kt)]
        ss = [_dot_nt(ck, qabs_ref[hp]) + bias2 for hp in range(npair)]
        m_prev = [m_ref[hp] for hp in range(npair)]
        m_new = [jnp.maximum(mp, jnp.max(_fold_rows(s, jnp.maximum), axis=0, keepdims=True))
                 for mp, s in zip(m_prev, ss)]
        ps = [jnp.exp(s - mn[0:1]).astype(BF16) for s, mn in zip(ss, m_new)]
        pv = [_dot(ckt, p) for p in ps]
        for hp in range(npair):
            acc_ref[hp] = acc_ref[hp] * jnp.exp(m_prev[hp][0:1] - m_new[hp][0:1]) + pv[hp]
            m_ref[hp] = m_new[hp]
        return carry

    lax.fori_loop(0, ntile, attn_tile, 0)

    out = jnp.zeros((Q, A_WIDTH), F32)
    for hp in range(npair):
        acc = acc_ref[hp]
        o_lat_t = acc[:R, :] / acc[R:R + 1, :]
        for j in range(2):
            o_lat = o_lat_t[:, j * Q:(j + 1) * Q].T.astype(BF16)
            out = out + _dot(o_lat, wuv_ref[2 * hp + j])
    o_ref[...] = out.astype(BF16)


def _dsa(pa3, kidx_keys, ckv_keys, ckvt_keys, wuk_pad, wuv_pad, k_sel, kt, interpret):
    bn, s, _ = pa3.shape
    lk = kidx_keys.shape[1]
    nqb = s // Q_BLOCK
    ntile_max = lk // kt
    npair = A_HEADS // 2
    rt = ckvt_keys.shape[1]
    tril = jnp.tril(jnp.ones((kt, kt), F32), k=-1).astype(BF16)
    full = lambda a: pl.BlockSpec(a.shape, lambda b, q: (0,) * a.ndim)
    kern = functools.partial(_dsa_kernel, k_sel=k_sel, kt=kt,
                             w_scale=(IDX_HEADS ** -0.5) * (IDX_DIM ** -0.5), att_scale=A_HEAD_DIM ** -0.5)
    return pl.pallas_call(
        kern,
        grid=(bn, nqb),
        in_specs=[pl.BlockSpec((None, Q_BLOCK, PA_COLS), lambda b, q: (b, q, 0)),
                  pl.BlockSpec((None, lk, LANES), lambda b, q: (b, 0, 0)),
                  pl.BlockSpec((None, lk, A_KV_RANK), lambda b, q: (b, 0, 0)),
                  pl.BlockSpec((None, rt, lk), lambda b, q: (b, 0, 0)),
                  full(wuk_pad), full(wuv_pad), full(tril)],
        out_specs=pl.BlockSpec((None, Q_BLOCK, A_WIDTH), lambda b, q: (b, q, 0)),
        out_shape=jax.ShapeDtypeStruct((bn, s, A_WIDTH), BF16),
        scratch_shapes=[pltpu.VMEM((ntile_max, kt, Q_BLOCK), F32),
                        pltpu.VMEM((npair, 2 * Q_BLOCK, A_KV_RANK), BF16),
                        pltpu.VMEM((npair, 2 * Q_BLOCK, LANES), BF16),
                        pltpu.VMEM((LANES, Q_BLOCK), F32),
                        pltpu.VMEM((npair, SUBLANES, 2 * Q_BLOCK), F32),
                        pltpu.VMEM((npair, rt, 2 * Q_BLOCK), F32)],
        compiler_params=_cparams(("parallel", "arbitrary")),
        interpret=interpret,
        name="dsa_attention",
    )(pa3, kidx_keys, ckv_keys, ckvt_keys, wuk_pad, wuv_pad, tril)


def _rwkv_prep_kernel(pb_ref, prev0_ref, mu_ref, w0_ref, a0_ref, kk_ref, ka_ref, w2a2_ref, g2_ref, hsum_ref,
                      r_ref, lw_ref, k_ref, v_ref, an_ref, b_ref, g_ref, carry_ref):
    t = pl.program_id(1)

    @pl.when(t == 0)
    def _():
        carry_ref[...] = prev0_ref[...]

    p = pb_ref[...]
    tb = p.shape[0]
    rolled = pltpu.roll(p, 1, 0)
    first = lax.broadcasted_iota(I32, p.shape, 0) == 0
    prev = jnp.where(first, jnp.broadcast_to(carry_ref[0:1, :], p.shape), rolled)
    carry_ref[0:1, :] = p[tb - 1:tb, :]
    xs = p + (prev - p) * mu_ref[...]
    W = B_WIDTH
    r, k, v = xs[:, :W], xs[:, W:2 * W], xs[:, 2 * W:3 * W]
    lora = xs[:, 3 * W:3 * W + LANES]
    lo_lane = lax.broadcasted_iota(I32, lora.shape, 1) < DECAY_LORA
    lora = jnp.where(lo_lane, jnp.tanh(lora), lora).astype(BF16)
    wa = _dot(lora, w2a2_ref[...])
    z = w0_ref[...] + wa[:, :W]
    w_log = -jax.nn.softplus(-z) - 0.5
    a = jax.nn.sigmoid(a0_ref[...] + wa[:, W:])
    gl = xs[:, 3 * W + LANES:]
    g = _dot(jax.nn.sigmoid(gl).astype(BF16), g2_ref[...])
    kk = k * kk_ref[...]
    ss = _mm_exact_rhs(kk * kk, hsum_ref[...])
    kk = kk * lax.rsqrt(ss + 1e-12)
    r_ref[...] = r
    lw_ref[...] = -jnp.exp(w_log)
    k_ref[...] = k * (1.0 + (a - 1.0) * ka_ref[...])
    v_ref[...] = v
    an_ref[...] = -kk
    b_ref[...] = kk * a
    g_ref[...] = g


def _rwkv_prep(pb3, prev0, mu, w0, a0, k_k, k_a, w2a2, g2, hsum, tb, interpret):
    bn, t, _ = pb3.shape
    full = lambda a: pl.BlockSpec(a.shape, lambda b, i: (0,) * a.ndim)
    outspec = pl.BlockSpec((None, tb, B_WIDTH), lambda b, i: (b, i, 0))
    return pl.pallas_call(
        _rwkv_prep_kernel,
        grid=(bn, t // tb),
        in_specs=[pl.BlockSpec((None, tb, B_COLS), lambda b, i: (b, i, 0)),
                  full(prev0), full(mu), full(w0), full(a0), full(k_k), full(k_a), full(w2a2), full(g2),
                  full(hsum)],
        out_specs=[outspec] * 7,
        out_shape=[jax.ShapeDtypeStruct((bn, t, B_WIDTH), F32)] * 7,
        scratch_shapes=[pltpu.VMEM((8, B_COLS), F32)],
        compiler_params=_cparams(("parallel", "arbitrary")),
        interpret=interpret,
        name="rwkv_prep",
    )(pb3, prev0, mu, w0, a0, k_k, k_a, w2a2, g2, hsum)


def _rwkv_scan_kernel(r_ref, lw_ref, k_ref, v_ref, an_ref, b_ref, s0_ref, ltri_ref, mstrict_ref, mincl_ref,
                      eye_ref, y_ref, sfin_ref, state_ref, *, n_batch, c):
    ci = pl.program_id(0)
    npair = B_HEADS // 2

    @pl.when(ci == 0)
    def _():
        state_ref[...] = s0_ref[...]

    lane = lax.broadcasted_iota(I32, (c, LANES), 1)
    left = lane < B_HEAD_DIM
    mstrict = mstrict_ref[...] > 0.5
    mincl = mincl_ref[...] > 0.5
    eye = eye_ref[...]

    def blockdiag(x):
        return jnp.concatenate([jnp.where(left, x, 0.0), jnp.where(left, 0.0, x)], axis=0)

    pairs = [(bi, pj) for bi in range(n_batch) for pj in range(npair)]
    at2, rt2, bte, kte, v2 = [], [], [], [], []
    a_ab, a_ak, a_rb, a_rk, p_end = [], [], [], [], []
    for bi, pj in pairs:
        ls = slice(pj * LANES, (pj + 1) * LANES)
        lw = lw_ref[bi, :, ls]
        cl = _mm_exact_lhs(ltri_ref[...], lw)
        p_inc = jnp.exp(cl)
        p_inv = jnp.exp(-cl)
        pe = p_inc[c - 1:c, :]
        a2 = blockdiag(an_ref[bi, :, ls] * jnp.exp(cl - lw))
        r2 = blockdiag(r_ref[bi, :, ls] * p_inc)
        b2 = blockdiag(b_ref[bi, :, ls] * p_inv)
        k2 = blockdiag(k_ref[bi, :, ls] * p_inv)
        quad = _mm1(jnp.concatenate([a2, r2], axis=0), jnp.concatenate([b2, k2], axis=0), _dot_nt)
        a_ab.append(jnp.where(mstrict, quad[:2 * c, :2 * c], 0.0))
        a_ak.append(jnp.where(mstrict, quad[:2 * c, 2 * c:], 0.0))
        a_rb.append(jnp.where(mincl, quad[2 * c:, :2 * c], 0.0))
        a_rk.append(jnp.where(mincl, quad[2 * c:, 2 * c:], 0.0))
        at2.append(a2)
        rt2.append(r2)
        bte.append(b2 * pe)
        kte.append(k2 * pe)
        v2.append(blockdiag(v_ref[bi, :, ls]))
        p_end.append(pe)

    tinv = [eye + m for m in a_ab]
    mpow = a_ab
    for _ in range(int(math.log2(c)) - 1):
        mpow = [_mm1(m, m) for m in mpow]
        tinv = [t + _mm1(t, m) for t, m in zip(tinv, mpow)]

    av = [_mm1(a, v) for a, v in zip(a_ak, v2)]
    tatv = [_mm1(t, jnp.concatenate([a, x], axis=1)) for t, a, x in zip(tinv, at2, av)]
    rb = [_mm1(a, x) for a, x in zip(a_rb, tatv)]
    hb = [_mm1(b, x, _dot_tn) for b, x in zip(bte, tatv)]
    rkv = [_mm1(a, v) for a, v in zip(a_rk, v2)]
    kv = [_mm1(k, v, _dot_tn) for k, v in zip(kte, v2)]

    for i, (bi, pj) in enumerate(pairs):
        ls = slice(pj * LANES, (pj + 1) * LANES)
        ra = rt2[i] + rb[i][:, :LANES]
        ha = eye * p_end[i] + hb[i][:, :LANES]
        hbd = state_ref[bi * npair + pj]
        nxt = _mm3(jnp.concatenate([ra, ha], axis=0), hbd)
        y2 = nxt[:2 * c] + rb[i][:, LANES:] + rkv[i]
        y_ref[bi, :, ls] = y2[:c] + y2[c:]
        state_ref[bi * npair + pj] = nxt[2 * c:] + hb[i][:, LANES:] + kv[i]

    @pl.when(ci == pl.num_programs(0) - 1)
    def _():
        sfin_ref[...] = state_ref[...]


def _rwkv_scan(r, lw, k, v, an, b, s0, c, interpret):
    bn, t, _ = r.shape
    npair = B_HEADS // 2
    idx = jnp.arange(2 * c)
    same = (idx[:, None] // c) == (idx[None, :] // c)
    mstrict = (same & ((idx[:, None] % c) > (idx[None, :] % c))).astype(F32)
    mincl = (same & ((idx[:, None] % c) >= (idx[None, :] % c))).astype(F32)
    ltri = jnp.tril(jnp.ones((c, c), F32)).astype(BF16)
    eye = jnp.eye(2 * c, dtype=F32)
    assert 2 * c == LANES
    full = lambda a: pl.BlockSpec(a.shape, lambda i: (0,) * a.ndim)
    seq = pl.BlockSpec((bn, c, B_WIDTH), lambda i: (0, i, 0))
    kern = functools.partial(_rwkv_scan_kernel, n_batch=bn, c=c)
    return pl.pallas_call(
        kern,
        grid=(t // c,),
        in_specs=[seq] * 6 + [full(s0), full(ltri), full(mstrict), full(mincl), full(eye)],
        out_specs=[seq, full(s0)],
        out_shape=[jax.ShapeDtypeStruct((bn, t, B_WIDTH), F32),
                   jax.ShapeDtypeStruct(s0.shape, F32)],
        scratch_shapes=[pltpu.VMEM(s0.shape, F32)],
        compiler_params=_cparams(("arbitrary",)),
        interpret=interpret,
        name="rwkv_scan",
    )(r, lw, k, v, an, b, s0, ltri, mstrict, mincl, eye)


def _rwkv_post_kernel(y_ref, r_ref, k_ref, v_ref, g_ref, rk_ref, lg_ref, lb_ref, hsum_ref, o_ref):
    y = y_ref[...]
    hs = hsum_ref[...]
    inv_n = 1.0 / B_HEAD_DIM
    mu = _mm_exact_rhs(y, hs) * inv_n
    d = y - mu
    var = _mm_exact_rhs(d * d, hs) * inv_n
    yn = d * lax.rsqrt(var + B_LNX_EPS) * lg_ref[...] + lb_ref[...]
    bonus = _mm_exact_rhs(r_ref[...] * k_ref[...] * rk_ref[...], hs) * v_ref[...]
    o_ref[...] = ((yn + bonus) * g_ref[...]).astype(BF16)


def _rwkv_post(y, r, k, v, g, r_k, lnx_g, lnx_b, hsum, tm, interpret):
    n = y.shape[0]
    full = lambda a: pl.BlockSpec(a.shape, lambda i: (0,) * a.ndim)
    row = pl.BlockSpec((tm, B_WIDTH), lambda i: (i, 0))
    return pl.pallas_call(
        _rwkv_post_kernel,
        grid=(n // tm,),
        in_specs=[row] * 5 + [full(r_k), full(lnx_g), full(lnx_b), full(hsum)],
        out_specs=row,
        out_shape=jax.ShapeDtypeStruct((n, B_WIDTH), BF16),
        compiler_params=_cparams(("parallel",)),
        interpret=interpret,
        name="rwkv_post",
    )(y, r, k, v, g, r_k, lnx_g, lnx_b, hsum)


def _mix_kernel(x_ref, oa_ref, ob_ref, pg_ref, lg_ref, lb_ref, bg_ref, wpa_ref, wpb_ref, wo_ref,
                l1g_ref, l1b_ref, h_ref, *, alpha):
    h0 = _layernorm(x_ref[...], lg_ref[...], lb_ref[...])
    gates = jax.nn.sigmoid(pg_ref[...] + bg_ref[...])
    mixed = (gates[:, :D_MODEL] * _dot(oa_ref[...], wpa_ref[...])
             + gates[:, D_MODEL:] * _dot(ob_ref[...], wpb_ref[...]))
    pre = alpha * h0 + _dot(mixed.astype(BF16), wo_ref[...])
    h_ref[...] = _layernorm(pre, l1g_ref[...], l1b_ref[...])


def _mix(x2d, oa, ob, pg, lg, lb, bg, wpa, wpb, wo, l1g, l1b, alpha, tm, interpret):
    n = x2d.shape[0]
    full = lambda a: pl.BlockSpec(a.shape, lambda i: (0,) * a.ndim)
    row = lambda c: pl.BlockSpec((tm, c), lambda i: (i, 0))
    return pl.pallas_call(
        functools.partial(_mix_kernel, alpha=alpha),
        grid=(n // tm,),
        in_specs=[row(D_MODEL), row(A_WIDTH), row(B_WIDTH), row(2 * D_MODEL),
                  full(lg), full(lb), full(bg), full(wpa), full(wpb), full(wo), full(l1g), full(l1b)],
        out_specs=row(D_MODEL),
        out_shape=jax.ShapeDtypeStruct((n, D_MODEL), F32),
        compiler_params=_cparams(("parallel",)),
        interpret=interpret,
        name="mix_out_proj",
    )(x2d, oa, ob, pg, lg, lb, bg, wpa, wpb, wo, l1g, l1b)


def _sort16_pairs():
    def merge(lo, hi, r):
        step = r * 2
        if step < hi - lo:
            yield from merge(lo, hi, step)
            yield from merge(lo + r, hi, step)
            yield from [(i, i + r) for i in range(lo + r, hi - r, step)]
        else:
            yield (lo, lo + r)

    def sort(lo, hi):
        if hi - lo >= 1:
            mid = lo + (hi - lo) // 2
            yield from sort(lo, mid)
            yield from sort(mid + 1, hi)
            yield from merge(lo, hi, 1)

    return tuple(sort(0, 15))


_SORT16 = _sort16_pairs()


def _ce(v, i, j):
    hi, lo = jnp.maximum(v[i], v[j]), jnp.minimum(v[i], v[j])
    v[i], v[j] = hi, lo


def _top16_sorted(st):
    v = [st[SUBLANES * k:SUBLANES * (k + 1), :] for k in range(16)]
    for i, j in _SORT16:
        _ce(v, i, j)
    for d in (4, 2, 1):
        w = [pltpu.roll(x, d, 0) for x in v]
        v = [jnp.maximum(v[k], w[15 - k]) for k in range(16)]
        for dist in (8, 4, 2, 1):
            for k in range(16):
                if not k & dist:
                    _ce(v, k, k + dist)
    return v


def _top16_ranked(st):
    x = st
    rank = jnp.full(st.shape, float(PEER_TOPK), F32)
    rows = []
    for m in range(PEER_TOPK):
        mx = jnp.max(x, axis=0, keepdims=True)
        hit = x == mx
        rank = jnp.where(hit, float(m), rank)
        x = jnp.where(hit, -jnp.inf, x)
        rows.append(mx)
    return rows, rank


def _rows_to_sublanes(rows, shape):
    sub = lax.broadcasted_iota(I32, shape, 0)
    out = jnp.broadcast_to(rows[-1], shape)
    for m in range(len(rows) - 2, -1, -1):
        out = jnp.where(sub == m, rows[m], out)
    return out


def _peer_route(s1, s2):
    tb = s1.shape[1]
    shape8 = (SUBLANES, tb)
    a = _top16_sorted(s1)
    b_rows, rank2 = _top16_ranked(s2)
    sub = lax.broadcasted_iota(I32, shape8, 0)
    b_lo = _rows_to_sublanes(b_rows[:8], shape8)
    b_hi = _rows_to_sublanes(b_rows[8:], shape8)
    a_hi = _rows_to_sublanes(a[8:], shape8)
    cands = [a[0] + b_lo, a[0] + b_hi, a[1] + b_lo]
    for i in range(2, 8):
        cands.append(jnp.where(sub < PEER_TOPK // (i + 1), a[i] + b_lo, -jnp.inf))
    cands.append(a_hi + b_rows[0])
    work = cands
    tau = None
    for r in range(PEER_TOPK):
        mx = work[0]
        for c in work[1:]:
            mx = jnp.maximum(mx, c)
        tau = jnp.max(mx, axis=0, keepdims=True)
        if r + 1 < PEER_TOPK:
            work = [jnp.where(c == tau, -jnp.inf, c) for c in work]
    cmax = a[0][0:1] + b_rows[0]
    z = jnp.zeros(shape8, F32)
    for c in cands:
        z = z + jnp.where(c >= tau, jnp.exp(c - cmax), 0.0)
    z = jnp.sum(z, axis=0, keepdims=True)
    count = jnp.zeros(s1.shape, F32)
    for m in range(4):
        count = count + jnp.where((s1 + b_rows[m]) >= tau, 1.0, 0.0)
    for i in range(3):
        extra = jnp.zeros(tau.shape, F32)
        for m in range(4, PEER_TOPK // (i + 1)):
            extra = extra + jnp.where((a[i][0:1] + b_rows[m]) >= tau, 1.0, 0.0)
        count = count + jnp.where(s1 == a[i][0:1], extra, 0.0)
    e1 = jnp.exp(s1 - a[0][0:1]) / z
    e2 = jnp.exp(s2 - b_rows[0])
    return count, e1, rank2, e2


def _peer_kernel(h_ref, wqt_ref, sk_ref, u_ref, vt_ref, l2g_ref, l2b_ref, o_ref,
                 hb_ref, st_ref, cnt_ref, e1_ref, r2_ref, e2_ref, acc_ref, *, alpha, eb):
    e = pl.program_id(1)
    PH = PEER_HEADS
    NK = PEER_NKEYS
    tb = h_ref.shape[0]
    pack = 2 * SUBLANES

    @pl.when(e == 0)
    def _():
        hb = h_ref[...].astype(BF16)
        hb_ref[...] = hb
        acc_ref[...] = jnp.zeros(acc_ref.shape, F32)
        for h in range(PH):
            for p in range(2):
                hp = 2 * h + p
                qt = _dot_nt(wqt_ref[hp * PEER_HALF:(hp + 1) * PEER_HALF, :], hb)
                st_ref[p] = _dot(sk_ref[hp], qt.astype(BF16))

            def route_tile(lt, carry, h=h):
                sl = pl.ds(pl.multiple_of(lt * LANES, LANES), LANES)
                count, e1, rank2, e2 = _peer_route(st_ref[0, :, sl], st_ref[1, :, sl])
                cnt_ref[h, :, sl] = count
                e1_ref[h, :, sl] = e1
                r2_ref[h, :, :, sl] = rank2.astype(BF16).reshape(NK // pack, pack, LANES)
                e2_ref[h, :, :, sl] = e2.astype(BF16).reshape(NK // pack, pack, LANES)
                return carry

            lax.fori_loop(0, tb // LANES, route_tile, 0)

    hb = hb_ref[...]
    sub_e = 2 * NK
    coefs = []
    for sc in range(eb // sub_e):
        gates = []
        for ii in range(sub_e // NK):
            irow = e * (eb // NK) + sc * (sub_e // NK) + ii
            gi = jnp.zeros((NK // pack, pack, tb), BF16)
            for h in range(PH):
                cb = jnp.broadcast_to(cnt_ref[h, pl.ds(irow, 1), :], (pack, tb)).astype(BF16)
                e1b = jnp.broadcast_to(e1_ref[h, pl.ds(irow, 1), :], (pack, tb)).astype(BF16)
                hit = r2_ref[h] < cb[None]
                gi = gi + jnp.where(hit, e2_ref[h] * e1b[None], jnp.zeros((), BF16))
            gates.append(gi.reshape(NK, tb))
        ht = _dot_nt(u_ref[sc * sub_e:(sc + 1) * sub_e, :], hb)
        act = (0.5 * ht * (1.0 + lax.erf(ht * (2.0 ** -0.5)))).astype(BF16)
        coefs.append(jnp.concatenate(gates, axis=0) * act)
    coef = jnp.concatenate(coefs, axis=0) if len(coefs) > 1 else coefs[0]
    acc_ref[...] += _dot(vt_ref[...], coef)

    @pl.when(e == pl.num_programs(1) - 1)
    def _():
        pre = alpha * h_ref[...] + acc_ref[...].T
        o_ref[...] = _layernorm(pre, l2g_ref[...], l2b_ref[...])


def _peer(h1, wqt, subkeys, u_bf, vt_bf, l2g, l2b, alpha, tb, eb, interpret):
    n = h1.shape[0]
    ne = u_bf.shape[0]
    pack = 2 * SUBLANES
    full = lambda a: pl.BlockSpec(a.shape, lambda i, e: (0,) * a.ndim, pipeline_mode=pl.Buffered(1))
    return pl.pallas_call(
        functools.partial(_peer_kernel, alpha=alpha, eb=eb),
        grid=(n // tb, ne // eb),
        in_specs=[pl.BlockSpec((tb, D_MODEL), lambda i, e: (i, 0)),
                  full(wqt), full(subkeys),
                  pl.BlockSpec((eb, D_MODEL), lambda i, e: (e, 0)),
                  pl.BlockSpec((None, D_MODEL, eb), lambda i, e: (e, 0, 0)),
                  full(l2g), full(l2b)],
        out_specs=pl.BlockSpec((tb, D_MODEL), lambda i, e: (i, 0)),
        out_shape=jax.ShapeDtypeStruct((n, D_MODEL), F32),
        scratch_shapes=[pltpu.VMEM((tb, D_MODEL), BF16),
                        pltpu.VMEM((2, PEER_NKEYS, tb), F32),
                        pltpu.VMEM((PEER_HEADS, PEER_NKEYS, tb), F32),
                        pltpu.VMEM((PEER_HEADS, PEER_NKEYS, tb), F32),
                        pltpu.VMEM((PEER_HEADS, PEER_NKEYS // pack, pack, tb), BF16),
                        pltpu.VMEM((PEER_HEADS, PEER_NKEYS // pack, pack, tb), BF16),
                        pltpu.VMEM((D_MODEL, tb), F32)],
        compiler_params=_cparams(("parallel", "arbitrary")),
        interpret=interpret,
        name="peer",
    )(h1, wqt, subkeys, u_bf, vt_bf, l2g, l2b)


def _row(v):
    return v.reshape(1, -1).astype(F32)


def _forward(x, meta, ln_in_g, ln_in_b, w_in, b_gate, a_kv_norm_g, a_w_uk, a_w_uv, a_kidx_g, a_kidx_b,
             b_mu, b_w0, b_w2, b_a0, b_a2, b_g2, b_k_k, b_k_a, b_r_k, b_lnx_g, b_lnx_b,
             w_pa, w_pb, w_o, ln1_g, ln1_b, peer_wq, peer_subkeys, peer_u, peer_v, ln2_g, ln2_b,
             interpret=False, tm=256, kt=512, peer_tb=1024, peer_eb=512):
    bn, s, d = x.shape
    assert s % Q_BLOCK == 0 and d == D_MODEL
    n = bn * s
    k_sel = min(TOPK_MAX, s // 4)
    alpha = (2.0 * DEPTH) ** 0.25
    l = 0
    tm = min(tm, n)

    w = w_in[l]
    o = 0
    aq = w[:, o:o + A_WIDTH]; o += A_WIDTH
    ackv = w[:, o:o + A_KV_RANK]; o += A_KV_RANK
    aqi = w[:, o:o + IDX_HEADS * IDX_DIM]; o += IDX_HEADS * IDX_DIM
    aki = w[:, o:o + IDX_DIM]; o += IDX_DIM
    awi = w[:, o:o + IDX_HEADS]; o += IDX_HEADS
    wbc = w[:, o:o + B_COLS]; o += B_COLS
    wgc = w[:, o:o + 2 * D_MODEL]
    misc = jnp.concatenate([aki, awi, jnp.zeros((d, LANES - IDX_DIM - IDX_HEADS), F32)], axis=1)
    aqi_p = jnp.pad(aqi.reshape(d, IDX_HEADS, IDX_DIM), ((0, 0), (0, 0), (0, LANES - IDX_DIM)))
    wa = jnp.concatenate([aq, ackv, misc, aqi_p.reshape(d, IDX_HEADS * LANES)], axis=1).astype(BF16)
    wb = wbc.astype(BF16)
    wg = wgc.astype(BF16)

    wuk = jnp.transpose(a_w_uk[l], (1, 2, 0))
    wuk_pad = jnp.zeros((A_HEADS, A_HEADS, A_HEAD_DIM, A_KV_RANK), F32)
    wuk_pad = wuk_pad.at[jnp.arange(A_HEADS), jnp.arange(A_HEADS)].set(wuk)
    wuk_pad = wuk_pad.reshape(A_HEADS, A_WIDTH, A_KV_RANK).astype(BF16)
    wuv = jnp.transpose(a_w_uv[l], (1, 0, 2))
    wuv_pad = jnp.zeros((A_HEADS, A_KV_RANK, A_HEADS, A_HEAD_DIM), F32)
    wuv_pad = wuv_pad.at[jnp.arange(A_HEADS), :, jnp.arange(A_HEADS)].set(wuv)
    wuv_pad = wuv_pad.reshape(A_HEADS, A_KV_RANK, A_WIDTH).astype(BF16)
    kig = jnp.pad(a_kidx_g[l], (0, LANES - IDX_DIM)).reshape(1, LANES)
    kib = jnp.pad(a_kidx_b[l], (0, LANES - IDX_DIM)).reshape(1, LANES)

    w2a2 = jnp.zeros((LANES, 2 * B_WIDTH), F32)
    w2a2 = w2a2.at[:DECAY_LORA, :B_WIDTH].set(b_w2[l]).at[DECAY_LORA:, B_WIDTH:].set(b_a2[l]).astype(BF16)
    hid = jnp.arange(B_WIDTH) // B_HEAD_DIM
    hsum = (hid[:, None] == hid[None, :]).astype(BF16)

    lg, lb = _row(ln_in_g), _row(ln_in_b)

    x2d = x.reshape(n, d)
    pa, pb, pg = _ln_proj(x2d, lg, lb, wa, wb, wg, tm, interpret)
    pa_m, pb_m, _ = _ln_proj(meta.astype(F32), lg, lb, wa, wb, wg, N_META, interpret)

    kvg = _row(a_kv_norm_g[l])
    ckv, kidx = _kv_prep(pa, kvg, kig, kib, tm, interpret)
    ckv_m, kidx_m = _kv_prep(pa_m, kvg, kig, kib, N_META, interpret)
    nblk_max = s // Q_BLOCK + 1
    lk = -(-nblk_max * LANES // kt) * kt

    def keys(meta_rows, frame_rows):
        c = frame_rows.shape[-1]
        fr = frame_rows.reshape(bn, s, c)
        mt = jnp.broadcast_to(meta_rows[None], (bn, N_META, c))
        return jnp.concatenate([mt, jnp.zeros((bn, LANES - N_META, c), BF16), fr,
                                jnp.zeros((bn, lk - LANES - s, c), BF16)], axis=1)

    ckv_keys = keys(ckv_m, ckv)
    ckvt_keys = jnp.concatenate([jnp.swapaxes(ckv_keys, 1, 2), jnp.ones((bn, 1, lk), BF16),
                                 jnp.zeros((bn, 2 * SUBLANES - 1, lk), BF16)], axis=1)
    o_a = _dsa(pa.reshape(bn, s, PA_COLS), keys(kidx_m, kidx), ckv_keys, ckvt_keys, wuk_pad, wuv_pad,
               k_sel, kt, interpret)

    mu, w0, a0 = _row(b_mu[l]), _row(b_w0[l]), _row(b_a0[l])
    k_k, k_a = _row(b_k_k[l]), _row(b_k_a[l])
    g2 = b_g2[l].astype(BF16)
    c = CHUNK
    zero_prev = jnp.zeros((8, B_COLS), F32)
    mr, mlw, mk, mv, man, mb, _ = _rwkv_prep(pb_m.reshape(1, N_META, B_COLS), zero_prev, mu, w0, a0, k_k, k_a,
                                             w2a2, g2, hsum, N_META, interpret)
    padm = lambda a: jnp.pad(a, ((0, 0), (c - N_META, 0), (0, 0)))
    s_zero = jnp.zeros((B_HEADS // 2, LANES, LANES), F32)
    _, s_meta = _rwkv_scan(padm(mr), padm(mlw), padm(mk), padm(mv), padm(man), padm(mb), s_zero, c, interpret)
    prev0 = jnp.broadcast_to(pb_m[N_META - 1:N_META], (8, B_COLS))
    tb_prep = min(tm, s)
    r, lw, k, v, an, b, g = _rwkv_prep(pb.reshape(bn, s, B_COLS), prev0, mu, w0, a0, k_k, k_a, w2a2, g2, hsum,
                                       tb_prep, interpret)
    y, _ = _rwkv_scan(r, lw, k, v, an, b, jnp.concatenate([s_meta] * bn, axis=0), c, interpret)
    f2 = lambda a: a.reshape(n, B_WIDTH)
    o_b = _rwkv_post(f2(y), f2(r), f2(k), f2(v), f2(g), _row(b_r_k[l]), _row(b_lnx_g[l]), _row(b_lnx_b[l]),
                     hsum, tm, interpret)

    h1 = _mix(x2d, o_a.reshape(n, A_WIDTH), o_b, pg, lg, lb, _row(b_gate[l]),
              w_pa[l].astype(BF16), w_pb[l].astype(BF16), w_o[l].astype(BF16),
              _row(ln1_g[l]), _row(ln1_b[l]), alpha, tm, interpret)

    wqt = peer_wq[l].T.astype(BF16)
    subk = peer_subkeys[l].reshape(PEER_HEADS * 2, PEER_NKEYS, PEER_HALF).astype(BF16)
    u_bf = peer_u[l].astype(BF16)
    vt_bf = jnp.swapaxes(peer_v[l].astype(BF16).reshape(-1, peer_eb, d), 1, 2)
    out = _peer(h1, wqt, subk, u_bf, vt_bf, _row(ln2_g[l]), _row(ln2_b[l]), alpha,
                min(peer_tb, n), peer_eb, interpret)
    return out.reshape(bn, s, d)


def kernel(x, meta, ln_in_g, ln_in_b, w_in, b_gate, a_kv_norm_g, a_w_uk, a_w_uv, a_kidx_g, a_kidx_b, b_mu, b_w0, b_w2, b_a0, b_a2, b_g2, b_k_k, b_k_a, b_r_k, b_lnx_g, b_lnx_b, w_pa, w_pb, w_o, ln1_g, ln1_b, peer_wq, peer_subkeys, peer_u, peer_v, ln2_g, ln2_b):
    return _forward(x, meta, ln_in_g, ln_in_b, w_in, b_gate, a_kv_norm_g, a_w_uk, a_w_uv, a_kidx_g, a_kidx_b,
                    b_mu, b_w0, b_w2, b_a0, b_a2, b_g2, b_k_k, b_k_a, b_r_k, b_lnx_g, b_lnx_b,
                    w_pa, w_pb, w_o, ln1_g, ln1_b, peer_wq, peer_subkeys, peer_u, peer_v, ln2_g, ln2_b)
```

```python
import functools
import math

import jax
import jax.numpy as jnp
from jax import lax
from jax.experimental import pallas as pl
from jax.experimental.pallas import tpu as pltpu

F32 = jnp.float32
BF16 = jnp.bfloat16
I32 = jnp.int32

D_MODEL = 1024
CHUNK = 64
N_META = 16
Q_BLOCK = 128
A_HEADS = 8
A_HEAD_DIM = 64
A_KV_RANK = 128
IDX_HEADS = 8
IDX_DIM = 64
TOPK_MAX = 256
B_HEADS = 8
B_HEAD_DIM = 64
DECAY_LORA = 64
AAA_LORA = 64
GATE_LORA = 128
B_LNX_EPS = 64e-5
PEER_HEADS = 8
PEER_NKEYS = 128
PEER_HALF = 128
PEER_TOPK = 16
A_WIDTH = A_HEADS * A_HEAD_DIM
B_WIDTH = B_HEADS * B_HEAD_DIM
B_COLS = 3 * B_WIDTH + DECAY_LORA + AAA_LORA + GATE_LORA
LN_EPS = 1e-5
DEPTH = 1

LANES = 128
SUBLANES = 8
VMEM_LIMIT = 56 * 1024 * 1024
PA_Q = 0
PA_CKV = PA_Q + A_WIDTH
PA_MISC = PA_CKV + A_KV_RANK
PA_QI = PA_MISC + LANES
PA_COLS = PA_QI + IDX_HEADS * LANES
NEG_BIG = -1e30


def _cparams(sem):
    return pltpu.CompilerParams(dimension_semantics=sem, vmem_limit_bytes=VMEM_LIMIT)


def _layernorm(x, g, b, eps=LN_EPS):
    mu = jnp.mean(x, axis=-1, keepdims=True)
    xc = x - mu
    var = jnp.mean(xc * xc, axis=-1, keepdims=True)
    return xc * lax.rsqrt(var + eps) * g + b


def _dot(a, b):
    return jnp.dot(a, b, preferred_element_type=F32)


def _dot_nt(a, b):
    return lax.dot_general(a, b, (((1,), (1,)), ((), ())), preferred_element_type=F32)


def _dot_tn(a, b):
    return lax.dot_general(a, b, (((0,), (0,)), ((), ())), preferred_element_type=F32)


def _split2(a):
    hi = a.astype(BF16)
    lo = (a - hi.astype(F32)).astype(BF16)
    return hi, lo


def _split3(a):
    hi = a.astype(BF16)
    r = a - hi.astype(F32)
    mid = r.astype(BF16)
    lo = (r - mid.astype(F32)).astype(BF16)
    return hi, mid, lo


def _mm3(a, b, dot=_dot):
    ah, al = _split2(a)
    bh, bl = _split2(b)
    return dot(ah, bh) + (dot(ah, bl) + dot(al, bh))


def _mm1(a, b, dot=_dot):
    return dot(a.astype(BF16), b.astype(BF16))


def _mm_exact_rhs(a, m, dot=_dot):
    h, mid, lo = _split3(a)
    return dot(h, m) + (dot(mid, m) + dot(lo, m))


def _mm_exact_lhs(m, a, dot=_dot):
    h, mid, lo = _split3(a)
    return dot(m, h) + (dot(m, mid) + dot(m, lo))


def _ln_proj_kernel(x_ref, g_ref, b_ref, wa_ref, wb_ref, wg_ref, pa_ref, pb_ref, pg_ref):
    h = _layernorm(x_ref[...], g_ref[...], b_ref[...]).astype(BF16)
    pa_ref[...] = _dot(h, wa_ref[...])
    pb_ref[...] = _dot(h, wb_ref[...])
    pg_ref[...] = _dot(h, wg_ref[...])


def _ln_proj(x2d, g, b, wa, wb, wg, tm, interpret):
    n, d = x2d.shape
    full = lambda a: pl.BlockSpec(a.shape, lambda i: (0,) * a.ndim)
    row = lambda c: pl.BlockSpec((tm, c), lambda i: (i, 0))
    return pl.pallas_call(
        _ln_proj_kernel,
        grid=(n // tm,),
        in_specs=[row(d), full(g), full(b), full(wa), full(wb), full(wg)],
        out_specs=[row(wa.shape[1]), row(wb.shape[1]), row(wg.shape[1])],
        out_shape=[jax.ShapeDtypeStruct((n, w.shape[1]), F32) for w in (wa, wb, wg)],
        compiler_params=_cparams(("parallel",)),
        interpret=interpret,
        name="ln_proj",
    )(x2d, g, b, wa, wb, wg)


def _kv_prep_kernel(p_ref, kvg_ref, kig_ref, kib_ref, ckv_ref, kidx_ref):
    blk = p_ref[...]
    c = blk[:, :LANES]
    cn = c * lax.rsqrt(jnp.mean(c * c, axis=-1, keepdims=True) + 1e-6) * kvg_ref[...]
    ckv_ref[...] = cn.astype(BF16)
    kx = blk[:, LANES:]
    valid = lax.broadcasted_iota(I32, kx.shape, 1) < IDX_DIM
    mu = jnp.sum(jnp.where(valid, kx, 0.0), axis=-1, keepdims=True) * (1.0 / IDX_DIM)
    dlt = jnp.where(valid, kx - mu, 0.0)
    var = jnp.sum(dlt * dlt, axis=-1, keepdims=True) * (1.0 / IDX_DIM)
    kn = dlt * lax.rsqrt(var + LN_EPS) * kig_ref[...] + kib_ref[...]
    kidx_ref[...] = jnp.where(valid, kn, 0.0).astype(BF16)


def _kv_prep(pa, kvg, kig, kib, tm, interpret):
    n = pa.shape[0]
    full = lambda a: pl.BlockSpec(a.shape, lambda i: (0,) * a.ndim)
    return pl.pallas_call(
        _kv_prep_kernel,
        grid=(n // tm,),
        in_specs=[pl.BlockSpec((tm, 2 * LANES), lambda i: (i, PA_CKV // (2 * LANES))),
                  full(kvg), full(kig), full(kib)],
        out_specs=[pl.BlockSpec((tm, LANES), lambda i: (i, 0)),
                   pl.BlockSpec((tm, LANES), lambda i: (i, 0))],
        out_shape=[jax.ShapeDtypeStruct((n, LANES), BF16),
                   jax.ShapeDtypeStruct((n, LANES), BF16)],
        compiler_params=_cparams(("parallel",)),
        interpret=interpret,
        name="dsa_kv_prep",
    )(pa, kvg, kig, kib)


def _fold_rows(x, op):
    while x.shape[0] > SUBLANES:
        half = x.shape[0] // 2
        x = op(x[:half], x[half:])
    return x


def _dsa_kernel(pa_ref, kidx_ref, ckv_ref, ckvt_ref, wuk_ref, wuv_ref, tril_ref, o_ref,
                sc_ref, qabs_ref, qi_ref, wt_ref, m_ref, acc_ref, *, k_sel, kt, w_scale, att_scale):
    qb = pl.program_id(1)
    Q = Q_BLOCK
    R = A_KV_RANK
    npair = A_HEADS // 2
    kb_per_tile = kt // LANES
    nblk = qb + 2
    ntile = (nblk + kb_per_tile - 1) // kb_per_tile

    pa = pa_ref[...]
    q = pa[:, PA_Q:PA_Q + A_WIDTH].astype(BF16)
    for h in range(A_HEADS):
        cols = slice((h % 2) * Q, (h % 2 + 1) * Q)
        qabs_ref[h // 2, :, cols] = (_dot(q, wuk_ref[h]) * att_scale).T.astype(BF16)
        qi_ref[h // 2, :, cols] = pa[:, PA_QI + h * LANES: PA_QI + (h + 1) * LANES].T.astype(BF16)
    wt_ref[...] = (pa[:, PA_MISC:PA_MISC + LANES] * w_scale).T

    key_row = lax.broadcasted_iota(I32, (kt, Q), 0)
    qry = lax.broadcasted_iota(I32, (1, Q), 1)
    rowlim = LANES + Q * qb + CHUNK * (qry // CHUNK + 1)

    def score_tile(t, carry):
        hi, lo = carry
        k0 = pl.multiple_of(t * kt, kt)
        kblk = kidx_ref[pl.ds(k0, kt), :]
        logits = [_dot(kblk, qi_ref[hp]) for hp in range(npair)]
        parts = []
        for hp, logit in enumerate(logits):
            logit = jnp.maximum(logit, 0.0)
            w0 = wt_ref[IDX_DIM + 2 * hp:IDX_DIM + 2 * hp + 1, :]
            w1 = wt_ref[IDX_DIM + 2 * hp + 1:IDX_DIM + 2 * hp + 2, :]
            parts.append(logit[:, :Q] * w0 + logit[:, Q:] * w1)
        score = (parts[0] + parts[1]) + (parts[2] + parts[3])
        g = key_row + k0
        adm = (g < N_META) | ((g >= LANES) & (g < rowlim))
        sc_ref[t] = jnp.where(adm, score, -jnp.inf)
        hi = jnp.maximum(hi, _fold_rows(jnp.where(adm, score, -jnp.inf), jnp.maximum))
        lo = jnp.minimum(lo, _fold_rows(jnp.where(adm, score, jnp.inf), jnp.minimum))
        return hi, lo

    hi8, lo8 = lax.fori_loop(0, ntile, score_tile, (jnp.full((SUBLANES, Q), -jnp.inf, F32),
                                                     jnp.full((SUBLANES, Q), jnp.inf, F32)))
    row_max = jnp.max(hi8, axis=0, keepdims=True)
    row_min = jnp.min(lo8, axis=0, keepdims=True)

    def count_ge(cand):
        def body(t, acc):
            hit = jnp.where(sc_ref[t] >= cand, 1.0, 0.0)
            return acc + _fold_rows(hit, jnp.add)

        acc = lax.fori_loop(0, ntile, body, jnp.zeros((SUBLANES, Q), F32))
        return jnp.sum(acc, axis=0, keepdims=True)

    def next_below(bound):
        def body(t, acc):
            sc = sc_ref[t]
            return jnp.maximum(acc, _fold_rows(jnp.where(sc < bound, sc, -jnp.inf), jnp.maximum))

        acc = lax.fori_loop(0, ntile, body, jnp.full((SUBLANES, Q), -jnp.inf, F32))
        return jnp.max(acc, axis=0, keepdims=True)

    kf = float(k_sel)
    take_all = 1e9
    n_adm = (N_META + Q * qb + CHUNK * (qry // CHUNK + 1)).astype(F32)
    c_max = count_ge(row_max)
    few = n_adm <= kf
    top_ties = jnp.logical_not(few) & (c_max >= kf)
    done = jnp.where(few | top_ties, 1.0, 0.0)
    thr = jnp.where(few, row_min, row_max)
    need = jnp.where(few, take_all, kf)
    n_eq = jnp.where(few, 0.0, c_max)

    def any_set(flag):
        return (jnp.max(jnp.where(flag, 1.0, 0.0)) > 0.0).astype(I32)

    def bisect(state):
        lo, hi, c_hi, thr, need, n_eq, done = state
        mid = lo + 0.5 * (hi - lo)
        live = (done < 0.5) & (mid > lo) & (mid < hi)
        c = count_ge(mid)
        found = live & (c == kf)
        up = live & (c > kf)
        down = live & (c < kf)
        state = (jnp.where(up, mid, lo), jnp.where(down, mid, hi), jnp.where(down, c, c_hi),
                 jnp.where(found, mid, thr), jnp.where(found, take_all, need), jnp.where(found, 0.0, n_eq),
                 jnp.where(found, 1.0, done))
        return state, up | down

    def bisect_twice(carry):
        state, _ = bisect(carry[:-1])
        state, moved = bisect(state)
        return state + (any_set(moved),)

    state = (row_min, row_max, c_max, thr, need, n_eq, done)
    state = lax.fori_loop(0, 12, lambda i, s: bisect(s)[0], state)
    state = lax.while_loop(lambda s: s[-1] > 0, bisect_twice, state + (any_set(done < 0.5),))
    _, hi, c_hi, thr, need, n_eq, done, _ = state

    def step_down(state):
        hi, c_hi, thr, need, n_eq, done, _ = state
        cand = next_below(hi)
        c = count_ge(cand)
        fin = (done < 0.5) & (c >= kf)
        go = (done < 0.5) & (c < kf)
        state = (jnp.where(go, cand, hi), jnp.where(go, c, c_hi), jnp.where(fin, cand, thr),
                 jnp.where(fin, kf - c_hi, need), jnp.where(fin, c - c_hi, n_eq), jnp.where(fin, 1.0, done))
        return state + (any_set(go),)

    state = (hi, c_hi, thr, need, n_eq, done, any_set(done < 0.5))
    _, _, thr, need, n_eq, _, _ = lax.while_loop(lambda s: s[-1] > 0, step_down, state)

    @pl.when(any_set(n_eq > need) > 0)
    def _():
        def demote(t, eq_before):
            sc = sc_ref[t]
            eq = sc == thr
            eqf = jnp.where(eq, 1.0, 0.0)
            prefix = _dot(tril_ref[...], eqf.astype(BF16))
            keep = (prefix + eq_before) < need
            sc_ref[t] = jnp.where(eq & jnp.logical_not(keep), -jnp.inf, sc)
            return eq_before + jnp.sum(eqf, axis=0, keepdims=True)

        lax.fori_loop(0, ntile, demote, jnp.zeros((1, Q), F32))

    m_ref[...] = jnp.full(m_ref.shape, NEG_BIG, F32)
    acc_ref[...] = jnp.zeros(acc_ref.shape, F32)

    def attn_tile(t, carry):
        k0 = pl.multiple_of(t * kt, kt)
        bias = jnp.where(sc_ref[t] >= thr, 0.0, NEG_BIG)
        bias2 = jnp.concatenate([bias, bias], axis=1)
        ck = ckv_ref[pl.ds(k0, kt), :]
        ckt = ckvt_ref[:, pl.ds(k0, kt)]
        ss = [_dot(ck, qabs_ref[hp]) + bias2 for hp in range(npair)]
        m_prev = [m_ref[hp] for hp in range(npair)]
        m_new = [jnp.maximum(mp, jnp.max(_fold_rows(s, jnp.maximum), axis=0, keepdims=True))
                 for mp, s in zip(m_prev, ss)]
        ps = [jnp.exp(s - mn[0:1]).astype(BF16) for s, mn in zip(ss, m_new)]
        pv = [_dot(ckt, p) for p in ps]
        for hp in range(npair):
            acc_ref[hp] = acc_ref[hp] * jnp.exp(m_prev[hp][0:1] - m_new[hp][0:1]) + pv[hp]
            m_ref[hp] = m_new[hp]
        return carry

    lax.fori_loop(0, ntile, attn_tile, 0)

    out = jnp.zeros((Q, A_WIDTH), F32)
    for hp in range(npair):
        acc = acc_ref[hp]
        o_lat_t = acc[:R, :] / acc[R:R + 1, :]
        for j in range(2):
            o_lat = o_lat_t[:, j * Q:(j + 1) * Q].T.astype(BF16)
            out = out + _dot(o_lat, wuv_ref[2 * hp + j])
    o_ref[...] = out.astype(BF16)


def _dsa(pa3, kidx_keys, ckv_keys, ckvt_keys, wuk_pad, wuv_pad, k_sel, kt, interpret):
    bn, s, _ = pa3.shape
    lk = kidx_keys.shape[1]
    nqb = s // Q_BLOCK
    ntile_max = lk // kt
    npair = A_HEADS // 2
    rt = ckvt_keys.shape[1]
    tril = jnp.tril(jnp.ones((kt, kt), F32), k=-1).astype(BF16)
    full = lambda a: pl.BlockSpec(a.shape, lambda b, q: (0,) * a.ndim)
    kern = functools.partial(_dsa_kernel, k_sel=k_sel, kt=kt,
                             w_scale=(IDX_HEADS ** -0.5) * (IDX_DIM ** -0.5), att_scale=A_HEAD_DIM ** -0.5)
    return pl.pallas_call(
        kern,
        grid=(bn, nqb),
        in_specs=[pl.BlockSpec((None, Q_BLOCK, PA_COLS), lambda b, q: (b, q, 0)),
                  pl.BlockSpec((None, lk, LANES), lambda b, q: (b, 0, 0)),
                  pl.BlockSpec((None, lk, A_KV_RANK), lambda b, q: (b, 0, 0)),
                  pl.BlockSpec((None, rt, lk), lambda b, q: (b, 0, 0)),
                  full(wuk_pad), full(wuv_pad), full(tril)],
        out_specs=pl.BlockSpec((None, Q_BLOCK, A_WIDTH), lambda b, q: (b, q, 0)),
        out_shape=jax.ShapeDtypeStruct((bn, s, A_WIDTH), BF16),
        scratch_shapes=[pltpu.VMEM((ntile_max, kt, Q_BLOCK), F32),
                        pltpu.VMEM((npair, A_KV_RANK, 2 * Q_BLOCK), BF16),
                        pltpu.VMEM((npair, LANES, 2 * Q_BLOCK), BF16),
                        pltpu.VMEM((LANES, Q_BLOCK), F32),
                        pltpu.VMEM((npair, SUBLANES, 2 * Q_BLOCK), F32),
                        pltpu.VMEM((npair, rt, 2 * Q_BLOCK), F32)],
        compiler_params=_cparams(("parallel", "arbitrary")),
        interpret=interpret,
        name="dsa_attention",
    )(pa3, kidx_keys, ckv_keys, ckvt_keys, wuk_pad, wuv_pad, tril)


def _rwkv_prep_kernel(pb_ref, prev0_ref, mu_ref, w0_ref, a0_ref, kk_ref, ka_ref, w2a2_ref, g2_ref, hsum_ref,
                      r_ref, lw_ref, k_ref, v_ref, an_ref, b_ref, g_ref, carry_ref):
    t = pl.program_id(1)

    @pl.when(t == 0)
    def _():
        carry_ref[...] = prev0_ref[...]

    p = pb_ref[...]
    tb = p.shape[0]
    rolled = pltpu.roll(p, 1, 0)
    first = lax.broadcasted_iota(I32, p.shape, 0) == 0
    prev = jnp.where(first, jnp.broadcast_to(carry_ref[0:1, :], p.shape), rolled)
    carry_ref[0:1, :] = p[tb - 1:tb, :]
    xs = p + (prev - p) * mu_ref[...]
    W = B_WIDTH
    r, k, v = xs[:, :W], xs[:, W:2 * W], xs[:, 2 * W:3 * W]
    lora = xs[:, 3 * W:3 * W + LANES]
    lo_lane = lax.broadcasted_iota(I32, lora.shape, 1) < DECAY_LORA
    lora = jnp.where(lo_lane, jnp.tanh(lora), lora).astype(BF16)
    wa = _dot(lora, w2a2_ref[...])
    z = w0_ref[...] + wa[:, :W]
    w_log = -jax.nn.softplus(-z) - 0.5
    a = jax.nn.sigmoid(a0_ref[...] + wa[:, W:])
    gl = xs[:, 3 * W + LANES:]
    g = _dot(jax.nn.sigmoid(gl).astype(BF16), g2_ref[...])
    kk = k * kk_ref[...]
    ss = _mm_exact_rhs(kk * kk, hsum_ref[...])
    kk = kk * lax.rsqrt(ss + 1e-12)
    r_ref[...] = r
    lw_ref[...] = -jnp.exp(w_log)
    k_ref[...] = k * (1.0 + (a - 1.0) * ka_ref[...])
    v_ref[...] = v
    an_ref[...] = -kk
    b_ref[...] = kk * a
    g_ref[...] = g


def _rwkv_prep(pb3, prev0, mu, w0, a0, k_k, k_a, w2a2, g2, hsum, tb, interpret):
    bn, t, _ = pb3.shape
    full = lambda a: pl.BlockSpec(a.shape, lambda b, i: (0,) * a.ndim)
    outspec = pl.BlockSpec((None, tb, B_WIDTH), lambda b, i: (b, i, 0))
    return pl.pallas_call(
        _rwkv_prep_kernel,
        grid=(bn, t // tb),
        in_specs=[pl.BlockSpec((None, tb, B_COLS), lambda b, i: (b, i, 0)),
                  full(prev0), full(mu), full(w0), full(a0), full(k_k), full(k_a), full(w2a2), full(g2),
                  full(hsum)],
        out_specs=[outspec] * 7,
        out_shape=[jax.ShapeDtypeStruct((bn, t, B_WIDTH), F32)] * 7,
        scratch_shapes=[pltpu.VMEM((8, B_COLS), F32)],
        compiler_params=_cparams(("parallel", "arbitrary")),
        interpret=interpret,
        name="rwkv_prep",
    )(pb3, prev0, mu, w0, a0, k_k, k_a, w2a2, g2, hsum)


def _rwkv_scan_kernel(r_ref, lw_ref, k_ref, v_ref, an_ref, b_ref, s0_ref, ltri_ref, mstrict_ref, mincl_ref,
                      eye_ref, y_ref, sfin_ref, state_ref, *, n_batch, c):
    ci = pl.program_id(0)
    npair = B_HEADS // 2

    @pl.when(ci == 0)
    def _():
        state_ref[...] = s0_ref[...]

    lane = lax.broadcasted_iota(I32, (c, LANES), 1)
    left = lane < B_HEAD_DIM
    mstrict = mstrict_ref[...] > 0.5
    mincl = mincl_ref[...] > 0.5
    eye = eye_ref[...]

    def blockdiag(x):
        return jnp.concatenate([jnp.where(left, x, 0.0), jnp.where(left, 0.0, x)], axis=0)

    pairs = [(bi, pj) for bi in range(n_batch) for pj in range(npair)]
    at2, rt2, bte, kte, v2 = [], [], [], [], []
    a_ab, a_ak, a_rb, a_rk, p_end = [], [], [], [], []
    for bi, pj in pairs:
        ls = slice(pj * LANES, (pj + 1) * LANES)
        lw = lw_ref[bi, :, ls]
        cl = _mm_exact_lhs(ltri_ref[...], lw)
        p_inc = jnp.exp(cl)
        p_inv = jnp.exp(-cl)
        pe = p_inc[c - 1:c, :]
        a2 = blockdiag(an_ref[bi, :, ls] * jnp.exp(cl - lw))
        r2 = blockdiag(r_ref[bi, :, ls] * p_inc)
        b2 = blockdiag(b_ref[bi, :, ls] * p_inv)
        k2 = blockdiag(k_ref[bi, :, ls] * p_inv)
        quad = _mm1(jnp.concatenate([a2, r2], axis=0), jnp.concatenate([b2, k2], axis=0), _dot_nt)
        a_ab.append(jnp.where(mstrict, quad[:2 * c, :2 * c], 0.0))
        a_ak.append(jnp.where(mstrict, quad[:2 * c, 2 * c:], 0.0))
        a_rb.append(jnp.where(mincl, quad[2 * c:, :2 * c], 0.0))
        a_rk.append(jnp.where(mincl, quad[2 * c:, 2 * c:], 0.0))
        at2.append(a2)
        rt2.append(r2)
        bte.append(b2 * pe)
        kte.append(k2 * pe)
        v2.append(blockdiag(v_ref[bi, :, ls]))
        p_end.append(pe)

    tinv = [eye + m for m in a_ab]
    mpow = a_ab
    for _ in range(int(math.log2(c)) - 1):
        mpow = [_mm1(m, m) for m in mpow]
        tinv = [t + _mm1(t, m) for t, m in zip(tinv, mpow)]

    av = [_mm1(a, v) for a, v in zip(a_ak, v2)]
    tatv = [_mm1(t, jnp.concatenate([a, x], axis=1)) for t, a, x in zip(tinv, at2, av)]
    rb = [_mm1(a, x) for a, x in zip(a_rb, tatv)]
    hb = [_mm1(b, x, _dot_tn) for b, x in zip(bte, tatv)]
    rkv = [_mm1(a, v) for a, v in zip(a_rk, v2)]
    kv = [_mm1(k, v, _dot_tn) for k, v in zip(kte, v2)]

    for i, (bi, pj) in enumerate(pairs):
        ls = slice(pj * LANES, (pj + 1) * LANES)
        ra = rt2[i] + rb[i][:, :LANES]
        ha = eye * p_end[i] + hb[i][:, :LANES]
        hbd = state_ref[bi * npair + pj]
        nxt = _mm3(jnp.concatenate([ra, ha], axis=0), hbd)
        y2 = nxt[:2 * c] + rb[i][:, LANES:] + rkv[i]
        y_ref[bi, :, ls] = y2[:c] + y2[c:]
        state_ref[bi * npair + pj] = nxt[2 * c:] + hb[i][:, LANES:] + kv[i]

    @pl.when(ci == pl.num_programs(0) - 1)
    def _():
        sfin_ref[...] = state_ref[...]


def _rwkv_scan(r, lw, k, v, an, b, s0, c, interpret):
    bn, t, _ = r.shape
    npair = B_HEADS // 2
    idx = jnp.arange(2 * c)
    same = (idx[:, None] // c) == (idx[None, :] // c)
    mstrict = (same & ((idx[:, None] % c) > (idx[None, :] % c))).astype(F32)
    mincl = (same & ((idx[:, None] % c) >= (idx[None, :] % c))).astype(F32)
    ltri = jnp.tril(jnp.ones((c, c), F32)).astype(BF16)
    eye = jnp.eye(2 * c, dtype=F32)
    assert 2 * c == LANES
    full = lambda a: pl.BlockSpec(a.shape, lambda i: (0,) * a.ndim)
    seq = pl.BlockSpec((bn, c, B_WIDTH), lambda i: (0, i, 0))
    kern = functools.partial(_rwkv_scan_kernel, n_batch=bn, c=c)
    return pl.pallas_call(
        kern,
        grid=(t // c,),
        in_specs=[seq] * 6 + [full(s0), full(ltri), full(mstrict), full(mincl), full(eye)],
        out_specs=[seq, full(s0)],
        out_shape=[jax.ShapeDtypeStruct((bn, t, B_WIDTH), F32),
                   jax.ShapeDtypeStruct(s0.shape, F32)],
        scratch_shapes=[pltpu.VMEM(s0.shape, F32)],
        compiler_params=_cparams(("arbitrary",)),
        interpret=interpret,
        name="rwkv_scan",
    )(r, lw, k, v, an, b, s0, ltri, mstrict, mincl, eye)


def _rwkv_post_kernel(y_ref, r_ref, k_ref, v_ref, g_ref, rk_ref, lg_ref, lb_ref, hsum_ref, o_ref):
    y = y_ref[...]
    hs = hsum_ref[...]
    inv_n = 1.0 / B_HEAD_DIM
    mu = _mm_exact_rhs(y, hs) * inv_n
    d = y - mu
    var = _mm_exact_rhs(d * d, hs) * inv_n
    yn = d * lax.rsqrt(var + B_LNX_EPS) * lg_ref[...] + lb_ref[...]
    bonus = _mm_exact_rhs(r_ref[...] * k_ref[...] * rk_ref[...], hs) * v_ref[...]
    o_ref[...] = ((yn + bonus) * g_ref[...]).astype(BF16)


def _rwkv_post(y, r, k, v, g, r_k, lnx_g, lnx_b, hsum, tm, interpret):
    n = y.shape[0]
    full = lambda a: pl.BlockSpec(a.shape, lambda i: (0,) * a.ndim)
    row = pl.BlockSpec((tm, B_WIDTH), lambda i: (i, 0))
    return pl.pallas_call(
        _rwkv_post_kernel,
        grid=(n // tm,),
        in_specs=[row] * 5 + [full(r_k), full(lnx_g), full(lnx_b), full(hsum)],
        out_specs=row,
        out_shape=jax.ShapeDtypeStruct((n, B_WIDTH), BF16),
        compiler_params=_cparams(("parallel",)),
        interpret=interpret,
        name="rwkv_post",
    )(y, r, k, v, g, r_k, lnx_g, lnx_b, hsum)


def _mix_kernel(x_ref, oa_ref, ob_ref, pg_ref, lg_ref, lb_ref, bg_ref, wpa_ref, wpb_ref, wo_ref,
                l1g_ref, l1b_ref, h_ref, *, alpha):
    h0 = _layernorm(x_ref[...], lg_ref[...], lb_ref[...])
    gates = jax.nn.sigmoid(pg_ref[...] + bg_ref[...])
    mixed = (gates[:, :D_MODEL] * _dot(oa_ref[...], wpa_ref[...])
             + gates[:, D_MODEL:] * _dot(ob_ref[...], wpb_ref[...]))
    pre = alpha * h0 + _dot(mixed.astype(BF16), wo_ref[...])
    h_ref[...] = _layernorm(pre, l1g_ref[...], l1b_ref[...])


def _mix(x2d, oa, ob, pg, lg, lb, bg, wpa, wpb, wo, l1g, l1b, alpha, tm, interpret):
    n = x2d.shape[0]
    full = lambda a: pl.BlockSpec(a.shape, lambda i: (0,) * a.ndim)
    row = lambda c: pl.BlockSpec((tm, c), lambda i: (i, 0))
    return pl.pallas_call(
        functools.partial(_mix_kernel, alpha=alpha),
        grid=(n // tm,),
        in_specs=[row(D_MODEL), row(A_WIDTH), row(B_WIDTH), row(2 * D_MODEL),
                  full(lg), full(lb), full(bg), full(wpa), full(wpb), full(wo), full(l1g), full(l1b)],
        out_specs=row(D_MODEL),
        out_shape=jax.ShapeDtypeStruct((n, D_MODEL), F32),
        compiler_params=_cparams(("parallel",)),
        interpret=interpret,
        name="mix_out_proj",
    )(x2d, oa, ob, pg, lg, lb, bg, wpa, wpb, wo, l1g, l1b)


def _sort16_pairs():
    def merge(lo, hi, r):
        step = r * 2
        if step < hi - lo:
            yield from merge(lo, hi, step)
            yield from merge(lo + r, hi, step)
            yield from [(i, i + r) for i in range(lo + r, hi - r, step)]
        else:
            yield (lo, lo + r)

    def sort(lo, hi):
        if hi - lo >= 1:
            mid = lo + (hi - lo) // 2
            yield from sort(lo, mid)
            yield from sort(mid + 1, hi)
            yield from merge(lo, hi, 1)

    return tuple(sort(0, 15))


_SORT16 = _sort16_pairs()


def _ce(v, i, j):
    hi, lo = jnp.maximum(v[i], v[j]), jnp.minimum(v[i], v[j])
    v[i], v[j] = hi, lo


def _top16_sorted(st):
    v = [st[SUBLANES * k:SUBLANES * (k + 1), :] for k in range(16)]
    for i, j in _SORT16:
        _ce(v, i, j)
    for d in (4, 2, 1):
        w = [pltpu.roll(x, d, 0) for x in v]
        v = [jnp.maximum(v[k], w[15 - k]) for k in range(16)]
        for dist in (8, 4, 2, 1):
            for k in range(16):
                if not k & dist:
                    _ce(v, k, k + dist)
    return v


def _top16_ranked(st):
    x = st
    rank = jnp.full(st.shape, float(PEER_TOPK), F32)
    rows = []
    for m in range(PEER_TOPK):
        mx = jnp.max(x, axis=0, keepdims=True)
        hit = x == mx
        rank = jnp.where(hit, float(m), rank)
        x = jnp.where(hit, -jnp.inf, x)
        rows.append(mx)
    return rows, rank


def _rows_to_sublanes(rows, shape):
    sub = lax.broadcasted_iota(I32, shape, 0)
    out = jnp.broadcast_to(rows[-1], shape)
    for m in range(len(rows) - 2, -1, -1):
        out = jnp.where(sub == m, rows[m], out)
    return out


def _peer_route(s1, s2):
    tb = s1.shape[1]
    shape8 = (SUBLANES, tb)
    a = _top16_sorted(s1)
    b_rows, rank2 = _top16_ranked(s2)
    sub = lax.broadcasted_iota(I32, shape8, 0)
    b_lo = _rows_to_sublanes(b_rows[:8], shape8)
    b_hi = _rows_to_sublanes(b_rows[8:], shape8)
    a_hi = _rows_to_sublanes(a[8:], shape8)
    cands = [a[0] + b_lo, a[0] + b_hi, a[1] + b_lo]
    for i in range(2, 8):
        cands.append(jnp.where(sub < PEER_TOPK // (i + 1), a[i] + b_lo, -jnp.inf))
    cands.append(a_hi + b_rows[0])
    work = cands
    tau = None
    for r in range(PEER_TOPK):
        mx = work[0]
        for c in work[1:]:
            mx = jnp.maximum(mx, c)
        tau = jnp.max(mx, axis=0, keepdims=True)
        if r + 1 < PEER_TOPK:
            work = [jnp.where(c == tau, -jnp.inf, c) for c in work]
    cmax = a[0][0:1] + b_rows[0]
    z = jnp.zeros(shape8, F32)
    for c in cands:
        z = z + jnp.where(c >= tau, jnp.exp(c - cmax), 0.0)
    z = jnp.sum(z, axis=0, keepdims=True)
    count = jnp.zeros(s1.shape, F32)
    for m in range(4):
        count = count + jnp.where((s1 + b_rows[m]) >= tau, 1.0, 0.0)
    for i in range(3):
        extra = jnp.zeros(tau.shape, F32)
        for m in range(4, PEER_TOPK // (i + 1)):
            extra = extra + jnp.where((a[i][0:1] + b_rows[m]) >= tau, 1.0, 0.0)
        count = count + jnp.where(s1 == a[i][0:1], extra, 0.0)
    e1 = jnp.exp(s1 - a[0][0:1]) / z
    e2 = jnp.exp(s2 - b_rows[0])
    return count, e1, rank2, e2


def _peer_kernel(h_ref, wqt_ref, sk_ref, u_ref, vt_ref, l2g_ref, l2b_ref, o_ref,
                 hbt_ref, st_ref, cnt_ref, e1_ref, r2_ref, e2_ref, acc_ref, *, alpha, eb):
    e = pl.program_id(1)
    PH = PEER_HEADS
    NK = PEER_NKEYS
    tb = h_ref.shape[0]
    pack = 2 * SUBLANES

    @pl.when(e == 0)
    def _():
        hbt_ref[...] = h_ref[...].T.astype(BF16)
        acc_ref[...] = jnp.zeros(acc_ref.shape, F32)
        for h in range(PH):
            if h % 2 == 0:
                rows = slice(2 * h * PEER_HALF, (2 * h + 4) * PEER_HALF)
                qt2 = _dot(wqt_ref[rows, :], hbt_ref[...]).astype(BF16)
            for p in range(2):
                hp = 2 * (h % 2) + p
                st_ref[p] = _dot(sk_ref[2 * h + p], qt2[hp * PEER_HALF:(hp + 1) * PEER_HALF])

            def route_tile(lt, carry, h=h):
                sl = pl.ds(pl.multiple_of(lt * LANES, LANES), LANES)
                count, e1, rank2, e2 = _peer_route(st_ref[0, :, sl], st_ref[1, :, sl])
                cnt_ref[h, :, sl] = count
                e1_ref[h, :, sl] = e1
                r2_ref[h, :, :, sl] = rank2.astype(BF16).reshape(NK // pack, pack, LANES)
                e2_ref[h, :, :, sl] = e2.astype(BF16).reshape(NK // pack, pack, LANES)
                return carry

            lax.fori_loop(0, tb // LANES, route_tile, 0)

    n_i = eb // NK
    gates = []
    for ii in range(n_i):
        irow = e * n_i + ii
        gi = jnp.zeros((NK // pack, pack, tb), BF16)
        for h in range(PH):
            cb = jnp.broadcast_to(cnt_ref[h, pl.ds(irow, 1), :], (pack, tb)).astype(BF16)
            e1b = jnp.broadcast_to(e1_ref[h, pl.ds(irow, 1), :], (pack, tb)).astype(BF16)
            hit = r2_ref[h] < cb[None]
            gi = gi + jnp.where(hit, e2_ref[h] * e1b[None], jnp.zeros((), BF16))
        gates.append(gi.reshape(NK, tb))
    ht = _dot(u_ref[...], hbt_ref[...])
    act = (0.5 * ht * (1.0 + lax.erf(ht * (2.0 ** -0.5)))).astype(BF16)
    coef = jnp.concatenate(gates, axis=0) * act
    acc_ref[...] += _dot(vt_ref[...], coef)

    @pl.when(e == pl.num_programs(1) - 1)
    def _():
        pre = alpha * h_ref[...] + acc_ref[...].T
        o_ref[...] = _layernorm(pre, l2g_ref[...], l2b_ref[...])


def _peer(h1, wqt, subkeys, u_bf, vt_bf, l2g, l2b, alpha, tb, eb, interpret):
    n = h1.shape[0]
    ne = u_bf.shape[0]
    pack = 2 * SUBLANES
    full = lambda a: pl.BlockSpec(a.shape, lambda i, e: (0,) * a.ndim, pipeline_mode=pl.Buffered(1))
    return pl.pallas_call(
        functools.partial(_peer_kernel, alpha=alpha, eb=eb),
        grid=(n // tb, ne // eb),
        in_specs=[pl.BlockSpec((tb, D_MODEL), lambda i, e: (i, 0), pipeline_mode=pl.Buffered(1)),
                  full(wqt), full(subkeys),
                  pl.BlockSpec((eb, D_MODEL), lambda i, e: (e, 0)),
                  pl.BlockSpec((None, D_MODEL, eb), lambda i, e: (e, 0, 0)),
                  full(l2g), full(l2b)],
        out_specs=pl.BlockSpec((tb, D_MODEL), lambda i, e: (i, 0)),
        out_shape=jax.ShapeDtypeStruct((n, D_MODEL), F32),
        scratch_shapes=[pltpu.VMEM((D_MODEL, tb), BF16),
                        pltpu.VMEM((2, PEER_NKEYS, tb), F32),
                        pltpu.VMEM((PEER_HEADS, PEER_NKEYS, tb), F32),
                        pltpu.VMEM((PEER_HEADS, PEER_NKEYS, tb), F32),
                        pltpu.VMEM((PEER_HEADS, PEER_NKEYS // pack, pack, tb), BF16),
                        pltpu.VMEM((PEER_HEADS, PEER_NKEYS // pack, pack, tb), BF16),
                        pltpu.VMEM((D_MODEL, tb), F32)],
        compiler_params=_cparams(("parallel", "arbitrary")),
        interpret=interpret,
        name="peer",
    )(h1, wqt, subkeys, u_bf, vt_bf, l2g, l2b)


def _row(v):
    return v.reshape(1, -1).astype(F32)


def _forward(x, meta, ln_in_g, ln_in_b, w_in, b_gate, a_kv_norm_g, a_w_uk, a_w_uv, a_kidx_g, a_kidx_b,
             b_mu, b_w0, b_w2, b_a0, b_a2, b_g2, b_k_k, b_k_a, b_r_k, b_lnx_g, b_lnx_b,
             w_pa, w_pb, w_o, ln1_g, ln1_b, peer_wq, peer_subkeys, peer_u, peer_v, ln2_g, ln2_b,
             interpret=False, tm=256, kt=512, peer_tb=1024, peer_eb=1024):
    bn, s, d = x.shape
    assert s % Q_BLOCK == 0 and d == D_MODEL
    n = bn * s
    k_sel = min(TOPK_MAX, s // 4)
    alpha = (2.0 * DEPTH) ** 0.25
    l = 0
    tm = min(tm, n)

    w = w_in[l]
    o = 0
    aq = w[:, o:o + A_WIDTH]; o += A_WIDTH
    ackv = w[:, o:o + A_KV_RANK]; o += A_KV_RANK
    aqi = w[:, o:o + IDX_HEADS * IDX_DIM]; o += IDX_HEADS * IDX_DIM
    aki = w[:, o:o + IDX_DIM]; o += IDX_DIM
    awi = w[:, o:o + IDX_HEADS]; o += IDX_HEADS
    wbc = w[:, o:o + B_COLS]; o += B_COLS
    wgc = w[:, o:o + 2 * D_MODEL]
    misc = jnp.concatenate([aki, awi, jnp.zeros((d, LANES - IDX_DIM - IDX_HEADS), F32)], axis=1)
    aqi_p = jnp.pad(aqi.reshape(d, IDX_HEADS, IDX_DIM), ((0, 0), (0, 0), (0, LANES - IDX_DIM)))
    wa = jnp.concatenate([aq, ackv, misc, aqi_p.reshape(d, IDX_HEADS * LANES)], axis=1).astype(BF16)
    wb = wbc.astype(BF16)
    wg = wgc.astype(BF16)

    wuk = jnp.transpose(a_w_uk[l], (1, 2, 0))
    wuk_pad = jnp.zeros((A_HEADS, A_HEADS, A_HEAD_DIM, A_KV_RANK), F32)
    wuk_pad = wuk_pad.at[jnp.arange(A_HEADS), jnp.arange(A_HEADS)].set(wuk)
    wuk_pad = wuk_pad.reshape(A_HEADS, A_WIDTH, A_KV_RANK).astype(BF16)
    wuv = jnp.transpose(a_w_uv[l], (1, 0, 2))
    wuv_pad = jnp.zeros((A_HEADS, A_KV_RANK, A_HEADS, A_HEAD_DIM), F32)
    wuv_pad = wuv_pad.at[jnp.arange(A_HEADS), :, jnp.arange(A_HEADS)].set(wuv)
    wuv_pad = wuv_pad.reshape(A_HEADS, A_KV_RANK, A_WIDTH).astype(BF16)
    kig = jnp.pad(a_kidx_g[l], (0, LANES - IDX_DIM)).reshape(1, LANES)
    kib = jnp.pad(a_kidx_b[l], (0, LANES - IDX_DIM)).reshape(1, LANES)

    w2a2 = jnp.zeros((LANES, 2 * B_WIDTH), F32)
    w2a2 = w2a2.at[:DECAY_LORA, :B_WIDTH].set(b_w2[l]).at[DECAY_LORA:, B_WIDTH:].set(b_a2[l]).astype(BF16)
    hid = jnp.arange(B_WIDTH) // B_HEAD_DIM
    hsum = (hid[:, None] == hid[None, :]).astype(BF16)

    lg, lb = _row(ln_in_g), _row(ln_in_b)

    x2d = x.reshape(n, d)
    pa, pb, pg = _ln_proj(x2d, lg, lb, wa, wb, wg, tm, interpret)
    pa_m, pb_m, _ = _ln_proj(meta.astype(F32), lg, lb, wa, wb, wg, N_META, interpret)

    kvg = _row(a_kv_norm_g[l])
    ckv, kidx = _kv_prep(pa, kvg, kig, kib, tm, interpret)
    ckv_m, kidx_m = _kv_prep(pa_m, kvg, kig, kib, N_META, interpret)
    nblk_max = s // Q_BLOCK + 1
    lk = -(-nblk_max * LANES // kt) * kt

    def keys(meta_rows, frame_rows):
        c = frame_rows.shape[-1]
        fr = frame_rows.reshape(bn, s, c)
        mt = jnp.broadcast_to(meta_rows[None], (bn, N_META, c))
        return jnp.concatenate([mt, jnp.zeros((bn, LANES - N_META, c), BF16), fr,
                                jnp.zeros((bn, lk - LANES - s, c), BF16)], axis=1)

    ckv_keys = keys(ckv_m, ckv)
    ckvt_keys = jnp.concatenate([jnp.swapaxes(ckv_keys, 1, 2), jnp.ones((bn, 1, lk), BF16),
                                 jnp.zeros((bn, 2 * SUBLANES - 1, lk), BF16)], axis=1)
    o_a = _dsa(pa.reshape(bn, s, PA_COLS), keys(kidx_m, kidx), ckv_keys, ckvt_keys, wuk_pad, wuv_pad,
               k_sel, kt, interpret)

    mu, w0, a0 = _row(b_mu[l]), _row(b_w0[l]), _row(b_a0[l])
    k_k, k_a = _row(b_k_k[l]), _row(b_k_a[l])
    g2 = b_g2[l].astype(BF16)
    c = CHUNK
    zero_prev = jnp.zeros((8, B_COLS), F32)
    mr, mlw, mk, mv, man, mb, _ = _rwkv_prep(pb_m.reshape(1, N_META, B_COLS), zero_prev, mu, w0, a0, k_k, k_a,
                                             w2a2, g2, hsum, N_META, interpret)
    padm = lambda a: jnp.pad(a, ((0, 0), (c - N_META, 0), (0, 0)))
    s_zero = jnp.zeros((B_HEADS // 2, LANES, LANES), F32)
    _, s_meta = _rwkv_scan(padm(mr), padm(mlw), padm(mk), padm(mv), padm(man), padm(mb), s_zero, c, interpret)
    prev0 = jnp.broadcast_to(pb_m[N_META - 1:N_META], (8, B_COLS))
    tb_prep = min(tm, s)
    r, lw, k, v, an, b, g = _rwkv_prep(pb.reshape(bn, s, B_COLS), prev0, mu, w0, a0, k_k, k_a, w2a2, g2, hsum,
                                       tb_prep, interpret)
    y, _ = _rwkv_scan(r, lw, k, v, an, b, jnp.concatenate([s_meta] * bn, axis=0), c, interpret)
    f2 = lambda a: a.reshape(n, B_WIDTH)
    o_b = _rwkv_post(f2(y), f2(r), f2(k), f2(v), f2(g), _row(b_r_k[l]), _row(b_lnx_g[l]), _row(b_lnx_b[l]),
                     hsum, tm, interpret)

    h1 = _mix(x2d, o_a.reshape(n, A_WIDTH), o_b, pg, lg, lb, _row(b_gate[l]),
              w_pa[l].astype(BF16), w_pb[l].astype(BF16), w_o[l].astype(BF16),
              _row(ln1_g[l]), _row(ln1_b[l]), alpha, tm, interpret)

    wqt = peer_wq[l].T.astype(BF16)
    subk = peer_subkeys[l].reshape(PEER_HEADS * 2, PEER_NKEYS, PEER_HALF).astype(BF16)
    u_bf = peer_u[l].astype(BF16)
    vt_bf = jnp.swapaxes(peer_v[l].astype(BF16).reshape(-1, peer_eb, d), 1, 2)
    out = _peer(h1, wqt, subk, u_bf, vt_bf, _row(ln2_g[l]), _row(ln2_b[l]), alpha,
                min(peer_tb, n), peer_eb, interpret)
    return out.reshape(bn, s, d)


def kernel(x, meta, ln_in_g, ln_in_b, w_in, b_gate, a_kv_norm_g, a_w_uk, a_w_uv, a_kidx_g, a_kidx_b, b_mu, b_w0, b_w2, b_a0, b_a2, b_g2, b_k_k, b_k_a, b_r_k, b_lnx_g, b_lnx_b, w_pa, w_pb, w_o, ln1_g, ln1_b, peer_wq, peer_subkeys, peer_u, peer_v, ln2_g, ln2_b):
    return _forward(x, meta, ln_in_g, ln_in_b, w_in, b_gate, a_kv_norm_g, a_w_uk, a_w_uv, a_kidx_g, a_kidx_b,
                    b_mu, b_w0, b_w2, b_a0, b_a2, b_g2, b_k_k, b_k_a, b_r_k, b_lnx_g, b_lnx_b,
                    w_pa, w_pb, w_o, ln1_g, ln1_b, peer_wq, peer_subkeys, peer_u, peer_v, ln2_g, ln2_b)
```

```python
import functools
import math

import jax
import jax.numpy as jnp
from jax import lax
from jax.experimental import pallas as pl
from jax.experimental.pallas import tpu as pltpu

F32 = jnp.float32
BF16 = jnp.bfloat16
I32 = jnp.int32

D_MODEL = 1024
CHUNK = 64
N_META = 16
Q_BLOCK = 128
A_HEADS = 8
A_HEAD_DIM = 64
A_KV_RANK = 128
IDX_HEADS = 8
IDX_DIM = 64
TOPK_MAX = 256
B_HEADS = 8
B_HEAD_DIM = 64
DECAY_LORA = 64
AAA_LORA = 64
GATE_LORA = 128
B_LNX_EPS = 64e-5
PEER_HEADS = 8
PEER_NKEYS = 128
PEER_HALF = 128
PEER_TOPK = 16
A_WIDTH = A_HEADS * A_HEAD_DIM
B_WIDTH = B_HEADS * B_HEAD_DIM
B_COLS = 3 * B_WIDTH + DECAY_LORA + AAA_LORA + GATE_LORA
LN_EPS = 1e-5
DEPTH = 1

LANES = 128
SUBLANES = 8
VMEM_LIMIT = 56 * 1024 * 1024
PA_Q = 0
PA_CKV = PA_Q + A_WIDTH
PA_MISC = PA_CKV + A_KV_RANK
PA_QI = PA_MISC + LANES
PA_COLS = PA_QI + IDX_HEADS * LANES
NEG_BIG = -1e30


def _cparams(sem):
    return pltpu.CompilerParams(dimension_semantics=sem, vmem_limit_bytes=VMEM_LIMIT)


def _layernorm(x, g, b, eps=LN_EPS):
    mu = jnp.mean(x, axis=-1, keepdims=True)
    xc = x - mu
    var = jnp.mean(xc * xc, axis=-1, keepdims=True)
    return xc * lax.rsqrt(var + eps) * g + b


def _dot(a, b):
    return jnp.dot(a, b, preferred_element_type=F32)


def _dot_nt(a, b):
    return lax.dot_general(a, b, (((1,), (1,)), ((), ())), preferred_element_type=F32)


def _dot_tn(a, b):
    return lax.dot_general(a, b, (((0,), (0,)), ((), ())), preferred_element_type=F32)


def _split2(a):
    hi = a.astype(BF16)
    lo = (a - hi.astype(F32)).astype(BF16)
    return hi, lo


def _split3(a):
    hi = a.astype(BF16)
    r = a - hi.astype(F32)
    mid = r.astype(BF16)
    lo = (r - mid.astype(F32)).astype(BF16)
    return hi, mid, lo


def _mm3(a, b, dot=_dot):
    ah, al = _split2(a)
    bh, bl = _split2(b)
    return dot(ah, bh) + (dot(ah, bl) + dot(al, bh))


def _mm1(a, b, dot=_dot):
    return dot(a.astype(BF16), b.astype(BF16))


def _mm_exact_rhs(a, m, dot=_dot):
    h, mid, lo = _split3(a)
    return dot(h, m) + (dot(mid, m) + dot(lo, m))


def _mm_exact_lhs(m, a, dot=_dot):
    h, mid, lo = _split3(a)
    return dot(m, h) + (dot(m, mid) + dot(m, lo))


def _ln_proj_kernel(x_ref, g_ref, b_ref, wa_ref, wb_ref, wg_ref, pa_ref, pb_ref, pg_ref):
    h = _layernorm(x_ref[...], g_ref[...], b_ref[...]).astype(BF16)
    pa_ref[...] = _dot(h, wa_ref[...])
    pb_ref[...] = _dot(h, wb_ref[...])
    pg_ref[...] = _dot(h, wg_ref[...])


def _ln_proj(x2d, g, b, wa, wb, wg, tm, interpret):
    n, d = x2d.shape
    full = lambda a: pl.BlockSpec(a.shape, lambda i: (0,) * a.ndim)
    row = lambda c: pl.BlockSpec((tm, c), lambda i: (i, 0))
    return pl.pallas_call(
        _ln_proj_kernel,
        grid=(n // tm,),
        in_specs=[row(d), full(g), full(b), full(wa), full(wb), full(wg)],
        out_specs=[row(wa.shape[1]), row(wb.shape[1]), row(wg.shape[1])],
        out_shape=[jax.ShapeDtypeStruct((n, w.shape[1]), F32) for w in (wa, wb, wg)],
        compiler_params=_cparams(("parallel",)),
        interpret=interpret,
        name="ln_proj",
    )(x2d, g, b, wa, wb, wg)


def _kv_prep_kernel(p_ref, kvg_ref, kig_ref, kib_ref, ckv_ref, kidx_ref):
    blk = p_ref[...]
    c = blk[:, :LANES]
    cn = c * lax.rsqrt(jnp.mean(c * c, axis=-1, keepdims=True) + 1e-6) * kvg_ref[...]
    ckv_ref[...] = cn.astype(BF16)
    kx = blk[:, LANES:]
    valid = lax.broadcasted_iota(I32, kx.shape, 1) < IDX_DIM
    mu = jnp.sum(jnp.where(valid, kx, 0.0), axis=-1, keepdims=True) * (1.0 / IDX_DIM)
    dlt = jnp.where(valid, kx - mu, 0.0)
    var = jnp.sum(dlt * dlt, axis=-1, keepdims=True) * (1.0 / IDX_DIM)
    kn = dlt * lax.rsqrt(var + LN_EPS) * kig_ref[...] + kib_ref[...]
    kidx_ref[...] = jnp.where(valid, kn, 0.0).astype(BF16)


def _kv_prep(pa, kvg, kig, kib, tm, interpret):
    n = pa.shape[0]
    full = lambda a: pl.BlockSpec(a.shape, lambda i: (0,) * a.ndim)
    return pl.pallas_call(
        _kv_prep_kernel,
        grid=(n // tm,),
        in_specs=[pl.BlockSpec((tm, 2 * LANES), lambda i: (i, PA_CKV // (2 * LANES))),
                  full(kvg), full(kig), full(kib)],
        out_specs=[pl.BlockSpec((tm, LANES), lambda i: (i, 0)),
                   pl.BlockSpec((tm, LANES), lambda i: (i, 0))],
        out_shape=[jax.ShapeDtypeStruct((n, LANES), BF16),
                   jax.ShapeDtypeStruct((n, LANES), BF16)],
        compiler_params=_cparams(("parallel",)),
        interpret=interpret,
        name="dsa_kv_prep",
    )(pa, kvg, kig, kib)


def _fold_rows(x, op):
    while x.shape[0] > SUBLANES:
        half = x.shape[0] // 2
        x = op(x[:half], x[half:])
    return x


def _dsa_kernel(pa_ref, kidx_ref, ckv_ref, ckvt_ref, wuk_ref, wuv_ref, tril_ref, o_ref,
                sc_ref, qabs_ref, qi_ref, wt_ref, m_ref, acc_ref, *, k_sel, kt, w_scale, att_scale):
    qb = pl.program_id(1)
    Q = Q_BLOCK
    R = A_KV_RANK
    npair = A_HEADS // 2
    kb_per_tile = kt // LANES
    nblk = qb + 2
    ntile = (nblk + kb_per_tile - 1) // kb_per_tile

    pa = pa_ref[...]
    q = pa[:, PA_Q:PA_Q + A_WIDTH].astype(BF16)
    for h in range(A_HEADS):
        cols = slice((h % 2) * Q, (h % 2 + 1) * Q)
        qabs_ref[h // 2, :, cols] = (_dot(q, wuk_ref[h]) * att_scale).T.astype(BF16)
        qi_ref[h // 2, :, cols] = pa[:, PA_QI + h * LANES: PA_QI + (h + 1) * LANES].T.astype(BF16)
    wt_ref[...] = (pa[:, PA_MISC:PA_MISC + LANES] * w_scale).T

    key_row = lax.broadcasted_iota(I32, (kt, Q), 0)
    qry = lax.broadcasted_iota(I32, (1, Q), 1)
    rowlim = LANES + Q * qb + CHUNK * (qry // CHUNK + 1)

    def score_tile(t, carry):
        hi, lo = carry
        k0 = pl.multiple_of(t * kt, kt)
        kblk = kidx_ref[pl.ds(k0, kt), :]
        logits = [_dot(kblk, qi_ref[hp]) for hp in range(npair)]
        parts = []
        for hp, logit in enumerate(logits):
            logit = jnp.maximum(logit, 0.0)
            w0 = wt_ref[IDX_DIM + 2 * hp:IDX_DIM + 2 * hp + 1, :]
            w1 = wt_ref[IDX_DIM + 2 * hp + 1:IDX_DIM + 2 * hp + 2, :]
            parts.append(logit[:, :Q] * w0 + logit[:, Q:] * w1)
        score = (parts[0] + parts[1]) + (parts[2] + parts[3])
        g = key_row + k0
        adm = (g < N_META) | ((g >= LANES) & (g < rowlim))
        sc_ref[t] = jnp.where(adm, score, -jnp.inf)
        hi = jnp.maximum(hi, _fold_rows(jnp.where(adm, score, -jnp.inf), jnp.maximum))
        lo = jnp.minimum(lo, _fold_rows(jnp.where(adm, score, jnp.inf), jnp.minimum))
        return hi, lo

    hi8, lo8 = lax.fori_loop(0, ntile, score_tile, (jnp.full((SUBLANES, Q), -jnp.inf, F32),
                                                     jnp.full((SUBLANES, Q), jnp.inf, F32)))
    row_max = jnp.max(hi8, axis=0, keepdims=True)
    row_min = jnp.min(lo8, axis=0, keepdims=True)

    def count_ge(cand):
        def body(t, acc):
            hit = jnp.where(sc_ref[t] >= cand, 1.0, 0.0)
            return acc + _fold_rows(hit, jnp.add)

        acc = lax.fori_loop(0, ntile, body, jnp.zeros((SUBLANES, Q), F32))
        return jnp.sum(acc, axis=0, keepdims=True)

    def next_below(bound):
        def body(t, acc):
            sc = sc_ref[t]
            return jnp.maximum(acc, _fold_rows(jnp.where(sc < bound, sc, -jnp.inf), jnp.maximum))

        acc = lax.fori_loop(0, ntile, body, jnp.full((SUBLANES, Q), -jnp.inf, F32))
        return jnp.max(acc, axis=0, keepdims=True)

    kf = float(k_sel)
    take_all = 1e9
    n_adm = (N_META + Q * qb + CHUNK * (qry // CHUNK + 1)).astype(F32)
    c_max = count_ge(row_max)
    few = n_adm <= kf
    top_ties = jnp.logical_not(few) & (c_max >= kf)
    done = jnp.where(few | top_ties, 1.0, 0.0)
    thr = jnp.where(few, row_min, row_max)
    need = jnp.where(few, take_all, kf)
    n_eq = jnp.where(few, 0.0, c_max)

    def any_set(flag):
        return (jnp.max(jnp.where(flag, 1.0, 0.0)) > 0.0).astype(I32)

    def bisect(state):
        lo, hi, c_hi, thr, need, n_eq, done = state
        mid = lo + 0.5 * (hi - lo)
        live = (done < 0.5) & (mid > lo) & (mid < hi)
        c = count_ge(mid)
        found = live & (c == kf)
        up = live & (c > kf)
        down = live & (c < kf)
        state = (jnp.where(up, mid, lo), jnp.where(down, mid, hi), jnp.where(down, c, c_hi),
                 jnp.where(found, mid, thr), jnp.where(found, take_all, need), jnp.where(found, 0.0, n_eq),
                 jnp.where(found, 1.0, done))
        return state, up | down

    def bisect_twice(carry):
        state, _ = bisect(carry[:-1])
        state, moved = bisect(state)
        return state + (any_set(moved),)

    state = (row_min, row_max, c_max, thr, need, n_eq, done)
    state = lax.fori_loop(0, 22, lambda i, s: bisect(s)[0], state)
    state = lax.while_loop(lambda s: s[-1] > 0, bisect_twice, state + (any_set(done < 0.5),))
    _, hi, c_hi, thr, need, n_eq, done, _ = state

    def step_down(state):
        hi, c_hi, thr, need, n_eq, done, _ = state
        cand = next_below(hi)
        c = count_ge(cand)
        fin = (done < 0.5) & (c >= kf)
        go = (done < 0.5) & (c < kf)
        state = (jnp.where(go, cand, hi), jnp.where(go, c, c_hi), jnp.where(fin, cand, thr),
                 jnp.where(fin, kf - c_hi, need), jnp.where(fin, c - c_hi, n_eq), jnp.where(fin, 1.0, done))
        return state + (any_set(go),)

    state = (hi, c_hi, thr, need, n_eq, done, any_set(done < 0.5))
    _, _, thr, need, n_eq, _, _ = lax.while_loop(lambda s: s[-1] > 0, step_down, state)

    @pl.when(any_set(n_eq > need) > 0)
    def _():
        def demote(t, eq_before):
            sc = sc_ref[t]
            eq = sc == thr
            eqf = jnp.where(eq, 1.0, 0.0)
            prefix = _dot(tril_ref[...], eqf.astype(BF16))
            keep = (prefix + eq_before) < need
            sc_ref[t] = jnp.where(eq & jnp.logical_not(keep), -jnp.inf, sc)
            return eq_before + jnp.sum(eqf, axis=0, keepdims=True)

        lax.fori_loop(0, ntile, demote, jnp.zeros((1, Q), F32))

    m_ref[...] = jnp.full(m_ref.shape, NEG_BIG, F32)
    acc_ref[...] = jnp.zeros(acc_ref.shape, F32)

    def attn_tile(t, carry):
        k0 = pl.multiple_of(t * kt, kt)
        bias = jnp.where(sc_ref[t] >= thr, 0.0, NEG_BIG)
        bias2 = jnp.concatenate([bias, bias], axis=1)
        ck = ckv_ref[pl.ds(k0, kt), :]
        ckt = ckvt_ref[:, pl.ds(k0, kt)]
        ss = [_dot(ck, qabs_ref[hp]) + bias2 for hp in range(npair)]
        m_prev = [m_ref[hp] for hp in range(npair)]
        m_new = [jnp.maximum(mp, jnp.max(_fold_rows(s, jnp.maximum), axis=0, keepdims=True))
                 for mp, s in zip(m_prev, ss)]
        ps = [jnp.exp(s - mn[0:1]).astype(BF16) for s, mn in zip(ss, m_new)]
        pv = [_dot(ckt, p) for p in ps]
        for hp in range(npair):
            acc_ref[hp] = acc_ref[hp] * jnp.exp(m_prev[hp][0:1] - m_new[hp][0:1]) + pv[hp]
            m_ref[hp] = m_new[hp]
        return carry

    lax.fori_loop(0, ntile, attn_tile, 0)

    out = jnp.zeros((Q, A_WIDTH), F32)
    for hp in range(npair):
        acc = acc_ref[hp]
        o_lat_t = acc[:R, :] / acc[R:R + 1, :]
        for j in range(2):
            o_lat = o_lat_t[:, j * Q:(j + 1) * Q].T.astype(BF16)
            out = out + _dot(o_lat, wuv_ref[2 * hp + j])
    o_ref[...] = out.astype(BF16)


def _dsa(pa3, kidx_keys, ckv_keys, ckvt_keys, wuk_pad, wuv_pad, k_sel, kt, interpret):
    bn, s, _ = pa3.shape
    lk = kidx_keys.shape[1]
    nqb = s // Q_BLOCK
    ntile_max = lk // kt
    npair = A_HEADS // 2
    rt = ckvt_keys.shape[1]
    tril = jnp.tril(jnp.ones((kt, kt), F32), k=-1).astype(BF16)
    full = lambda a: pl.BlockSpec(a.shape, lambda b, q: (0,) * a.ndim)
    kern = functools.partial(_dsa_kernel, k_sel=k_sel, kt=kt,
                             w_scale=(IDX_HEADS ** -0.5) * (IDX_DIM ** -0.5), att_scale=A_HEAD_DIM ** -0.5)
    return pl.pallas_call(
        kern,
        grid=(bn, nqb),
        in_specs=[pl.BlockSpec((None, Q_BLOCK, PA_COLS), lambda b, q: (b, q, 0)),
                  pl.BlockSpec((None, lk, LANES), lambda b, q: (b, 0, 0)),
                  pl.BlockSpec((None, lk, A_KV_RANK), lambda b, q: (b, 0, 0)),
                  pl.BlockSpec((None, rt, lk), lambda b, q: (b, 0, 0)),
                  full(wuk_pad), full(wuv_pad), full(tril)],
        out_specs=pl.BlockSpec((None, Q_BLOCK, A_WIDTH), lambda b, q: (b, q, 0)),
        out_shape=jax.ShapeDtypeStruct((bn, s, A_WIDTH), BF16),
        scratch_shapes=[pltpu.VMEM((ntile_max, kt, Q_BLOCK), F32),
                        pltpu.VMEM((npair, A_KV_RANK, 2 * Q_BLOCK), BF16),
                        pltpu.VMEM((npair, LANES, 2 * Q_BLOCK), BF16),
                        pltpu.VMEM((LANES, Q_BLOCK), F32),
                        pltpu.VMEM((npair, SUBLANES, 2 * Q_BLOCK), F32),
                        pltpu.VMEM((npair, rt, 2 * Q_BLOCK), F32)],
        compiler_params=_cparams(("parallel", "arbitrary")),
        interpret=interpret,
        name="dsa_attention",
    )(pa3, kidx_keys, ckv_keys, ckvt_keys, wuk_pad, wuv_pad, tril)


def _rwkv_prep_kernel(pb_ref, prev0_ref, mu_ref, w0_ref, a0_ref, kk_ref, ka_ref, w2a2_ref, g2_ref, hsum_ref,
                      r_ref, lw_ref, k_ref, v_ref, an_ref, b_ref, g_ref, carry_ref):
    t = pl.program_id(1)

    @pl.when(t == 0)
    def _():
        carry_ref[...] = prev0_ref[...]

    p = pb_ref[...]
    tb = p.shape[0]
    rolled = pltpu.roll(p, 1, 0)
    first = lax.broadcasted_iota(I32, p.shape, 0) == 0
    prev = jnp.where(first, jnp.broadcast_to(carry_ref[0:1, :], p.shape), rolled)
    carry_ref[0:1, :] = p[tb - 1:tb, :]
    xs = p + (prev - p) * mu_ref[...]
    W = B_WIDTH
    r, k, v = xs[:, :W], xs[:, W:2 * W], xs[:, 2 * W:3 * W]
    lora = xs[:, 3 * W:3 * W + LANES]
    lo_lane = lax.broadcasted_iota(I32, lora.shape, 1) < DECAY_LORA
    lora = jnp.where(lo_lane, jnp.tanh(lora), lora).astype(BF16)
    wa = _dot(lora, w2a2_ref[...])
    z = w0_ref[...] + wa[:, :W]
    w_log = -jax.nn.softplus(-z) - 0.5
    a = jax.nn.sigmoid(a0_ref[...] + wa[:, W:])
    gl = xs[:, 3 * W + LANES:]
    g = _dot(jax.nn.sigmoid(gl).astype(BF16), g2_ref[...])
    kk = k * kk_ref[...]
    ss = _mm_exact_rhs(kk * kk, hsum_ref[...])
    kk = kk * lax.rsqrt(ss + 1e-12)
    r_ref[...] = r
    lw_ref[...] = -jnp.exp(w_log)
    k_ref[...] = k * (1.0 + (a - 1.0) * ka_ref[...])
    v_ref[...] = v
    an_ref[...] = -kk
    b_ref[...] = kk * a
    g_ref[...] = g


def _rwkv_prep(pb3, prev0, mu, w0, a0, k_k, k_a, w2a2, g2, hsum, tb, interpret):
    bn, t, _ = pb3.shape
    full = lambda a: pl.BlockSpec(a.shape, lambda b, i: (0,) * a.ndim)
    outspec = pl.BlockSpec((None, tb, B_WIDTH), lambda b, i: (b, i, 0))
    return pl.pallas_call(
        _rwkv_prep_kernel,
        grid=(bn, t // tb),
        in_specs=[pl.BlockSpec((None, tb, B_COLS), lambda b, i: (b, i, 0)),
                  full(prev0), full(mu), full(w0), full(a0), full(k_k), full(k_a), full(w2a2), full(g2),
                  full(hsum)],
        out_specs=[outspec] * 7,
        out_shape=[jax.ShapeDtypeStruct((bn, t, B_WIDTH), F32)] * 7,
        scratch_shapes=[pltpu.VMEM((8, B_COLS), F32)],
        compiler_params=_cparams(("parallel", "arbitrary")),
        interpret=interpret,
        name="rwkv_prep",
    )(pb3, prev0, mu, w0, a0, k_k, k_a, w2a2, g2, hsum)


def _rwkv_scan_kernel(r_ref, lw_ref, k_ref, v_ref, an_ref, b_ref, s0_ref, ltri_ref, mstrict_ref, mincl_ref,
                      eye_ref, y_ref, sfin_ref, state_ref, *, n_batch, c):
    ci = pl.program_id(0)
    npair = B_HEADS // 2

    @pl.when(ci == 0)
    def _():
        state_ref[...] = s0_ref[...]

    lane = lax.broadcasted_iota(I32, (c, LANES), 1)
    left = lane < B_HEAD_DIM
    mstrict = mstrict_ref[...] > 0.5
    mincl = mincl_ref[...] > 0.5
    eye = eye_ref[...]

    def blockdiag(x):
        return jnp.concatenate([jnp.where(left, x, 0.0), jnp.where(left, 0.0, x)], axis=0)

    pairs = [(bi, pj) for bi in range(n_batch) for pj in range(npair)]
    at2, rt2, bte, kte, v2 = [], [], [], [], []
    a_ab, a_ak, a_rb, a_rk, p_end = [], [], [], [], []
    for bi, pj in pairs:
        ls = slice(pj * LANES, (pj + 1) * LANES)
        lw = lw_ref[bi, :, ls]
        cl = _mm_exact_lhs(ltri_ref[...], lw)
        p_inc = jnp.exp(cl)
        p_inv = jnp.exp(-cl)
        pe = p_inc[c - 1:c, :]
        a2 = blockdiag(an_ref[bi, :, ls] * jnp.exp(cl - lw))
        r2 = blockdiag(r_ref[bi, :, ls] * p_inc)
        b2 = blockdiag(b_ref[bi, :, ls] * p_inv)
        k2 = blockdiag(k_ref[bi, :, ls] * p_inv)
        quad = _mm1(jnp.concatenate([a2, r2], axis=0), jnp.concatenate([b2, k2], axis=0), _dot_nt)
        a_ab.append(jnp.where(mstrict, quad[:2 * c, :2 * c], 0.0))
        a_ak.append(jnp.where(mstrict, quad[:2 * c, 2 * c:], 0.0))
        a_rb.append(jnp.where(mincl, quad[2 * c:, :2 * c], 0.0))
        a_rk.append(jnp.where(mincl, quad[2 * c:, 2 * c:], 0.0))
        at2.append(a2)
        rt2.append(r2)
        bte.append(b2 * pe)
        kte.append(k2 * pe)
        v2.append(blockdiag(v_ref[bi, :, ls]))
        p_end.append(pe)

    tinv = [eye + m for m in a_ab]
    mpow = a_ab
    for _ in range(int(math.log2(c)) - 1):
        mpow = [_mm1(m, m) for m in mpow]
        tinv = [t + _mm1(t, m) for t, m in zip(tinv, mpow)]

    av = [_mm1(a, v) for a, v in zip(a_ak, v2)]
    tatv = [_mm1(t, jnp.concatenate([a, x], axis=1)) for t, a, x in zip(tinv, at2, av)]
    rb = [_mm1(a, x) for a, x in zip(a_rb, tatv)]
    hb = [_mm1(b, x, _dot_tn) for b, x in zip(bte, tatv)]
    rkv = [_mm1(a, v) for a, v in zip(a_rk, v2)]
    kv = [_mm1(k, v, _dot_tn) for k, v in zip(kte, v2)]

    for i, (bi, pj) in enumerate(pairs):
        ls = slice(pj * LANES, (pj + 1) * LANES)
        ra = rt2[i] + rb[i][:, :LANES]
        ha = eye * p_end[i] + hb[i][:, :LANES]
        hbd = state_ref[bi * npair + pj]
        nxt = _mm3(jnp.concatenate([ra, ha], axis=0), hbd)
        y2 = nxt[:2 * c] + rb[i][:, LANES:] + rkv[i]
        y_ref[bi, :, ls] = y2[:c] + y2[c:]
        state_ref[bi * npair + pj] = nxt[2 * c:] + hb[i][:, LANES:] + kv[i]

    @pl.when(ci == pl.num_programs(0) - 1)
    def _():
        sfin_ref[...] = state_ref[...]


def _rwkv_scan(r, lw, k, v, an, b, s0, c, interpret):
    bn, t, _ = r.shape
    npair = B_HEADS // 2
    idx = jnp.arange(2 * c)
    same = (idx[:, None] // c) == (idx[None, :] // c)
    mstrict = (same & ((idx[:, None] % c) > (idx[None, :] % c))).astype(F32)
    mincl = (same & ((idx[:, None] % c) >= (idx[None, :] % c))).astype(F32)
    ltri = jnp.tril(jnp.ones((c, c), F32)).astype(BF16)
    eye = jnp.eye(2 * c, dtype=F32)
    assert 2 * c == LANES
    full = lambda a: pl.BlockSpec(a.shape, lambda i: (0,) * a.ndim)
    seq = pl.BlockSpec((bn, c, B_WIDTH), lambda i: (0, i, 0))
    kern = functools.partial(_rwkv_scan_kernel, n_batch=bn, c=c)
    return pl.pallas_call(
        kern,
        grid=(t // c,),
        in_specs=[seq] * 6 + [full(s0), full(ltri), full(mstrict), full(mincl), full(eye)],
        out_specs=[seq, full(s0)],
        out_shape=[jax.ShapeDtypeStruct((bn, t, B_WIDTH), F32),
                   jax.ShapeDtypeStruct(s0.shape, F32)],
        scratch_shapes=[pltpu.VMEM(s0.shape, F32)],
        compiler_params=_cparams(("arbitrary",)),
        interpret=interpret,
        name="rwkv_scan",
    )(r, lw, k, v, an, b, s0, ltri, mstrict, mincl, eye)


def _rwkv_post_kernel(y_ref, r_ref, k_ref, v_ref, g_ref, rk_ref, lg_ref, lb_ref, hsum_ref, o_ref):
    y = y_ref[...]
    hs = hsum_ref[...]
    inv_n = 1.0 / B_HEAD_DIM
    mu = _mm_exact_rhs(y, hs) * inv_n
    d = y - mu
    var = _mm_exact_rhs(d * d, hs) * inv_n
    yn = d * lax.rsqrt(var + B_LNX_EPS) * lg_ref[...] + lb_ref[...]
    bonus = _mm_exact_rhs(r_ref[...] * k_ref[...] * rk_ref[...], hs) * v_ref[...]
    o_ref[...] = ((yn + bonus) * g_ref[...]).astype(BF16)


def _rwkv_post(y, r, k, v, g, r_k, lnx_g, lnx_b, hsum, tm, interpret):
    n = y.shape[0]
    full = lambda a: pl.BlockSpec(a.shape, lambda i: (0,) * a.ndim)
    row = pl.BlockSpec((tm, B_WIDTH), lambda i: (i, 0))
    return pl.pallas_call(
        _rwkv_post_kernel,
        grid=(n // tm,),
        in_specs=[row] * 5 + [full(r_k), full(lnx_g), full(lnx_b), full(hsum)],
        out_specs=row,
        out_shape=jax.ShapeDtypeStruct((n, B_WIDTH), BF16),
        compiler_params=_cparams(("parallel",)),
        interpret=interpret,
        name="rwkv_post",
    )(y, r, k, v, g, r_k, lnx_g, lnx_b, hsum)


def _mix_kernel(x_ref, oa_ref, ob_ref, pg_ref, lg_ref, lb_ref, bg_ref, wpa_ref, wpb_ref, wo_ref,
                l1g_ref, l1b_ref, h_ref, *, alpha):
    h0 = _layernorm(x_ref[...], lg_ref[...], lb_ref[...])
    gates = jax.nn.sigmoid(pg_ref[...] + bg_ref[...])
    mixed = (gates[:, :D_MODEL] * _dot(oa_ref[...], wpa_ref[...])
             + gates[:, D_MODEL:] * _dot(ob_ref[...], wpb_ref[...]))
    pre = alpha * h0 + _dot(mixed.astype(BF16), wo_ref[...])
    h_ref[...] = _layernorm(pre, l1g_ref[...], l1b_ref[...])


def _mix(x2d, oa, ob, pg, lg, lb, bg, wpa, wpb, wo, l1g, l1b, alpha, tm, interpret):
    n = x2d.shape[0]
    full = lambda a: pl.BlockSpec(a.shape, lambda i: (0,) * a.ndim)
    row = lambda c: pl.BlockSpec((tm, c), lambda i: (i, 0))
    return pl.pallas_call(
        functools.partial(_mix_kernel, alpha=alpha),
        grid=(n // tm,),
        in_specs=[row(D_MODEL), row(A_WIDTH), row(B_WIDTH), row(2 * D_MODEL),
                  full(lg), full(lb), full(bg), full(wpa), full(wpb), full(wo), full(l1g), full(l1b)],
        out_specs=row(D_MODEL),
        out_shape=jax.ShapeDtypeStruct((n, D_MODEL), F32),
        compiler_params=_cparams(("parallel",)),
        interpret=interpret,
        name="mix_out_proj",
    )(x2d, oa, ob, pg, lg, lb, bg, wpa, wpb, wo, l1g, l1b)


def _sort16_pairs():
    def merge(lo, hi, r):
        step = r * 2
        if step < hi - lo:
            yield from merge(lo, hi, step)
            yield from merge(lo + r, hi, step)
            yield from [(i, i + r) for i in range(lo + r, hi - r, step)]
        else:
            yield (lo, lo + r)

    def sort(lo, hi):
        if hi - lo >= 1:
            mid = lo + (hi - lo) // 2
            yield from sort(lo, mid)
            yield from sort(mid + 1, hi)
            yield from merge(lo, hi, 1)

    return tuple(sort(0, 15))


_SORT16 = _sort16_pairs()


def _ce(v, i, j):
    hi, lo = jnp.maximum(v[i], v[j]), jnp.minimum(v[i], v[j])
    v[i], v[j] = hi, lo


def _top16_sorted(st):
    v = [st[SUBLANES * k:SUBLANES * (k + 1), :] for k in range(16)]
    for i, j in _SORT16:
        _ce(v, i, j)
    for d in (4, 2, 1):
        w = [pltpu.roll(x, d, 0) for x in v]
        v = [jnp.maximum(v[k], w[15 - k]) for k in range(16)]
        for dist in (8, 4, 2, 1):
            for k in range(16):
                if not k & dist:
                    _ce(v, k, k + dist)
    return v


def _top16_ranked(st):
    x = st
    rank = jnp.full(st.shape, float(PEER_TOPK), F32)
    rows = []
    for m in range(PEER_TOPK):
        mx = jnp.max(x, axis=0, keepdims=True)
        hit = x == mx
        rank = jnp.where(hit, float(m), rank)
        x = jnp.where(hit, -jnp.inf, x)
        rows.append(mx)
    return rows, rank


def _rows_to_sublanes(rows, shape):
    sub = lax.broadcasted_iota(I32, shape, 0)
    out = jnp.broadcast_to(rows[-1], shape)
    for m in range(len(rows) - 2, -1, -1):
        out = jnp.where(sub == m, rows[m], out)
    return out


def _peer_route(s1, s2):
    tb = s1.shape[1]
    shape8 = (SUBLANES, tb)
    a = _top16_sorted(s1)
    b_rows, rank2 = _top16_ranked(s2)
    sub = lax.broadcasted_iota(I32, shape8, 0)
    b_lo = _rows_to_sublanes(b_rows[:8], shape8)
    b_hi = _rows_to_sublanes(b_rows[8:], shape8)
    a_hi = _rows_to_sublanes(a[8:], shape8)
    cands = [a[0] + b_lo, a[0] + b_hi, a[1] + b_lo]
    for i in range(2, 8):
        cands.append(jnp.where(sub < PEER_TOPK // (i + 1), a[i] + b_lo, -jnp.inf))
    cands.append(a_hi + b_rows[0])
    work = cands
    tau = None
    for r in range(PEER_TOPK):
        mx = work[0]
        for c in work[1:]:
            mx = jnp.maximum(mx, c)
        tau = jnp.max(mx, axis=0, keepdims=True)
        if r + 1 < PEER_TOPK:
            work = [jnp.where(c == tau, -jnp.inf, c) for c in work]
    cmax = a[0][0:1] + b_rows[0]
    z = jnp.zeros(shape8, F32)
    for c in cands:
        z = z + jnp.where(c >= tau, jnp.exp(c - cmax), 0.0)
    z = jnp.sum(z, axis=0, keepdims=True)
    count = jnp.zeros(s1.shape, F32)
    for m in range(4):
        count = count + jnp.where((s1 + b_rows[m]) >= tau, 1.0, 0.0)
    for i in range(3):
        extra = jnp.zeros(tau.shape, F32)
        for m in range(4, PEER_TOPK // (i + 1)):
            extra = extra + jnp.where((a[i][0:1] + b_rows[m]) >= tau, 1.0, 0.0)
        count = count + jnp.where(s1 == a[i][0:1], extra, 0.0)
    e1 = jnp.exp(s1 - a[0][0:1]) / z
    e2 = jnp.exp(s2 - b_rows[0])
    return count, e1, rank2, e2


def _peer_kernel(h_ref, wqt_ref, sk_ref, u_ref, vt_ref, l2g_ref, l2b_ref, o_ref,
                 hbt_ref, st_ref, cnt_ref, e1_ref, r2_ref, e2_ref, acc_ref, *, alpha, eb):
    e = pl.program_id(1)
    PH = PEER_HEADS
    NK = PEER_NKEYS
    tb = h_ref.shape[0]
    pack = 2 * SUBLANES

    @pl.when(e == 0)
    def _():
        hbt_ref[...] = h_ref[...].T.astype(BF16)
        acc_ref[...] = jnp.zeros(acc_ref.shape, F32)
        for h in range(PH):
            if h % 2 == 0:
                rows = slice(2 * h * PEER_HALF, (2 * h + 4) * PEER_HALF)
                qt2 = _dot(wqt_ref[rows, :], hbt_ref[...]).astype(BF16)
            for p in range(2):
                hp = 2 * (h % 2) + p
                st_ref[p] = _dot(sk_ref[2 * h + p], qt2[hp * PEER_HALF:(hp + 1) * PEER_HALF])

            def route_tile(lt, carry, h=h):
                sl = pl.ds(pl.multiple_of(lt * LANES, LANES), LANES)
                count, e1, rank2, e2 = _peer_route(st_ref[0, :, sl], st_ref[1, :, sl])
                cnt_ref[h, :, sl] = count
                e1_ref[h, :, sl] = e1
                r2_ref[h, :, :, sl] = rank2.astype(BF16).reshape(NK // pack, pack, LANES)
                e2_ref[h, :, :, sl] = e2.astype(BF16).reshape(NK // pack, pack, LANES)
                return carry

            lax.fori_loop(0, tb // LANES, route_tile, 0)

    n_i = eb // NK
    gates = []
    for ii in range(n_i):
        irow = e * n_i + ii
        gi = jnp.zeros((NK // pack, pack, tb), BF16)
        for h in range(PH):
            cb = jnp.broadcast_to(cnt_ref[h, pl.ds(irow, 1), :], (pack, tb)).astype(BF16)
            e1b = jnp.broadcast_to(e1_ref[h, pl.ds(irow, 1), :], (pack, tb)).astype(BF16)
            hit = r2_ref[h] < cb[None]
            gi = gi + jnp.where(hit, e2_ref[h] * e1b[None], jnp.zeros((), BF16))
        gates.append(gi.reshape(NK, tb))
    ht = _dot(u_ref[...], hbt_ref[...])
    act = (0.5 * ht * (1.0 + lax.erf(ht * (2.0 ** -0.5)))).astype(BF16)
    coef = jnp.concatenate(gates, axis=0) * act
    acc_ref[...] += _dot(vt_ref[...], coef)

    @pl.when(e == pl.num_programs(1) - 1)
    def _():
        pre = alpha * h_ref[...] + acc_ref[...].T
        o_ref[...] = _layernorm(pre, l2g_ref[...], l2b_ref[...])


def _peer(h1, wqt, subkeys, u_bf, vt_bf, l2g, l2b, alpha, tb, eb, interpret):
    n = h1.shape[0]
    ne = u_bf.shape[0]
    pack = 2 * SUBLANES
    full = lambda a: pl.BlockSpec(a.shape, lambda i, e: (0,) * a.ndim, pipeline_mode=pl.Buffered(1))
    return pl.pallas_call(
        functools.partial(_peer_kernel, alpha=alpha, eb=eb),
        grid=(n // tb, ne // eb),
        in_specs=[pl.BlockSpec((tb, D_MODEL), lambda i, e: (i, 0), pipeline_mode=pl.Buffered(1)),
                  full(wqt), full(subkeys),
                  pl.BlockSpec((eb, D_MODEL), lambda i, e: (e, 0)),
                  pl.BlockSpec((None, D_MODEL, eb), lambda i, e: (e, 0, 0)),
                  full(l2g), full(l2b)],
        out_specs=pl.BlockSpec((tb, D_MODEL), lambda i, e: (i, 0)),
        out_shape=jax.ShapeDtypeStruct((n, D_MODEL), F32),
        scratch_shapes=[pltpu.VMEM((D_MODEL, tb), BF16),
                        pltpu.VMEM((2, PEER_NKEYS, tb), F32),
                        pltpu.VMEM((PEER_HEADS, PEER_NKEYS, tb), F32),
                        pltpu.VMEM((PEER_HEADS, PEER_NKEYS, tb), F32),
                        pltpu.VMEM((PEER_HEADS, PEER_NKEYS // pack, pack, tb), BF16),
                        pltpu.VMEM((PEER_HEADS, PEER_NKEYS // pack, pack, tb), BF16),
                        pltpu.VMEM((D_MODEL, tb), F32)],
        compiler_params=_cparams(("parallel", "arbitrary")),
        interpret=interpret,
        name="peer",
    )(h1, wqt, subkeys, u_bf, vt_bf, l2g, l2b)


def _row(v):
    return v.reshape(1, -1).astype(F32)


def _forward(x, meta, ln_in_g, ln_in_b, w_in, b_gate, a_kv_norm_g, a_w_uk, a_w_uv, a_kidx_g, a_kidx_b,
             b_mu, b_w0, b_w2, b_a0, b_a2, b_g2, b_k_k, b_k_a, b_r_k, b_lnx_g, b_lnx_b,
             w_pa, w_pb, w_o, ln1_g, ln1_b, peer_wq, peer_subkeys, peer_u, peer_v, ln2_g, ln2_b,
             interpret=False, tm=256, kt=512, peer_tb=1024, peer_eb=1024):
    bn, s, d = x.shape
    assert s % Q_BLOCK == 0 and d == D_MODEL
    n = bn * s
    k_sel = min(TOPK_MAX, s // 4)
    alpha = (2.0 * DEPTH) ** 0.25
    l = 0
    tm = min(tm, n)

    w = w_in[l]
    o = 0
    aq = w[:, o:o + A_WIDTH]; o += A_WIDTH
    ackv = w[:, o:o + A_KV_RANK]; o += A_KV_RANK
    aqi = w[:, o:o + IDX_HEADS * IDX_DIM]; o += IDX_HEADS * IDX_DIM
    aki = w[:, o:o + IDX_DIM]; o += IDX_DIM
    awi = w[:, o:o + IDX_HEADS]; o += IDX_HEADS
    wbc = w[:, o:o + B_COLS]; o += B_COLS
    wgc = w[:, o:o + 2 * D_MODEL]
    misc = jnp.concatenate([aki, awi, jnp.zeros((d, LANES - IDX_DIM - IDX_HEADS), F32)], axis=1)
    aqi_p = jnp.pad(aqi.reshape(d, IDX_HEADS, IDX_DIM), ((0, 0), (0, 0), (0, LANES - IDX_DIM)))
    wa = jnp.concatenate([aq, ackv, misc, aqi_p.reshape(d, IDX_HEADS * LANES)], axis=1).astype(BF16)
    wb = wbc.astype(BF16)
    wg = wgc.astype(BF16)

    wuk = jnp.transpose(a_w_uk[l], (1, 2, 0))
    wuk_pad = jnp.zeros((A_HEADS, A_HEADS, A_HEAD_DIM, A_KV_RANK), F32)
    wuk_pad = wuk_pad.at[jnp.arange(A_HEADS), jnp.arange(A_HEADS)].set(wuk)
    wuk_pad = wuk_pad.reshape(A_HEADS, A_WIDTH, A_KV_RANK).astype(BF16)
    wuv = jnp.transpose(a_w_uv[l], (1, 0, 2))
    wuv_pad = jnp.zeros((A_HEADS, A_KV_RANK, A_HEADS, A_HEAD_DIM), F32)
    wuv_pad = wuv_pad.at[jnp.arange(A_HEADS), :, jnp.arange(A_HEADS)].set(wuv)
    wuv_pad = wuv_pad.reshape(A_HEADS, A_KV_RANK, A_WIDTH).astype(BF16)
    kig = jnp.pad(a_kidx_g[l], (0, LANES - IDX_DIM)).reshape(1, LANES)
    kib = jnp.pad(a_kidx_b[l], (0, LANES - IDX_DIM)).reshape(1, LANES)

    w2a2 = jnp.zeros((LANES, 2 * B_WIDTH), F32)
    w2a2 = w2a2.at[:DECAY_LORA, :B_WIDTH].set(b_w2[l]).at[DECAY_LORA:, B_WIDTH:].set(b_a2[l]).astype(BF16)
    hid = jnp.arange(B_WIDTH) // B_HEAD_DIM
    hsum = (hid[:, None] == hid[None, :]).astype(BF16)

    lg, lb = _row(ln_in_g), _row(ln_in_b)

    x2d = x.reshape(n, d)
    pa, pb, pg = _ln_proj(x2d, lg, lb, wa, wb, wg, tm, interpret)
    pa_m, pb_m, _ = _ln_proj(meta.astype(F32), lg, lb, wa, wb, wg, N_META, interpret)

    kvg = _row(a_kv_norm_g[l])
    ckv, kidx = _kv_prep(pa, kvg, kig, kib, tm, interpret)
    ckv_m, kidx_m = _kv_prep(pa_m, kvg, kig, kib, N_META, interpret)
    nblk_max = s // Q_BLOCK + 1
    lk = -(-nblk_max * LANES // kt) * kt

    def keys(meta_rows, frame_rows):
        c = frame_rows.shape[-1]
        fr = frame_rows.reshape(bn, s, c)
        mt = jnp.broadcast_to(meta_rows[None], (bn, N_META, c))
        return jnp.concatenate([mt, jnp.zeros((bn, LANES - N_META, c), BF16), fr,
                                jnp.zeros((bn, lk - LANES - s, c), BF16)], axis=1)

    ckv_keys = keys(ckv_m, ckv)
    ckvt_keys = jnp.concatenate([jnp.swapaxes(ckv_keys, 1, 2), jnp.ones((bn, 1, lk), BF16),
                                 jnp.zeros((bn, 2 * SUBLANES - 1, lk), BF16)], axis=1)
    o_a = _dsa(pa.reshape(bn, s, PA_COLS), keys(kidx_m, kidx), ckv_keys, ckvt_keys, wuk_pad, wuv_pad,
               k_sel, kt, interpret)

    mu, w0, a0 = _row(b_mu[l]), _row(b_w0[l]), _row(b_a0[l])
    k_k, k_a = _row(b_k_k[l]), _row(b_k_a[l])
    g2 = b_g2[l].astype(BF16)
    c = CHUNK
    zero_prev = jnp.zeros((8, B_COLS), F32)
    mr, mlw, mk, mv, man, mb, _ = _rwkv_prep(pb_m.reshape(1, N_META, B_COLS), zero_prev, mu, w0, a0, k_k, k_a,
                                             w2a2, g2, hsum, N_META, interpret)
    padm = lambda a: jnp.pad(a, ((0, 0), (c - N_META, 0), (0, 0)))
    s_zero = jnp.zeros((B_HEADS // 2, LANES, LANES), F32)
    _, s_meta = _rwkv_scan(padm(mr), padm(mlw), padm(mk), padm(mv), padm(man), padm(mb), s_zero, c, interpret)
    prev0 = jnp.broadcast_to(pb_m[N_META - 1:N_META], (8, B_COLS))
    tb_prep = min(tm, s)
    r, lw, k, v, an, b, g = _rwkv_prep(pb.reshape(bn, s, B_COLS), prev0, mu, w0, a0, k_k, k_a, w2a2, g2, hsum,
                                       tb_prep, interpret)
    y, _ = _rwkv_scan(r, lw, k, v, an, b, jnp.concatenate([s_meta] * bn, axis=0), c, interpret)
    f2 = lambda a: a.reshape(n, B_WIDTH)
    o_b = _rwkv_post(f2(y), f2(r), f2(k), f2(v), f2(g), _row(b_r_k[l]), _row(b_lnx_g[l]), _row(b_lnx_b[l]),
                     hsum, tm, interpret)

    h1 = _mix(x2d, o_a.reshape(n, A_WIDTH), o_b, pg, lg, lb, _row(b_gate[l]),
              w_pa[l].astype(BF16), w_pb[l].astype(BF16), w_o[l].astype(BF16),
              _row(ln1_g[l]), _row(ln1_b[l]), alpha, tm, interpret)

    wqt = peer_wq[l].T.astype(BF16)
    subk = peer_subkeys[l].reshape(PEER_HEADS * 2, PEER_NKEYS, PEER_HALF).astype(BF16)
    u_bf = peer_u[l].astype(BF16)
    vt_bf = jnp.swapaxes(peer_v[l].astype(BF16).reshape(-1, peer_eb, d), 1, 2)
    out = _peer(h1, wqt, subk, u_bf, vt_bf, _row(ln2_g[l]), _row(ln2_b[l]), alpha,
                min(peer_tb, n), peer_eb, interpret)
    return out.reshape(bn, s, d)


def kernel(x, meta, ln_in_g, ln_in_b, w_in, b_gate, a_kv_norm_g, a_w_uk, a_w_uv, a_kidx_g, a_kidx_b, b_mu, b_w0, b_w2, b_a0, b_a2, b_g2, b_k_k, b_k_a, b_r_k, b_lnx_g, b_lnx_b, w_pa, w_pb, w_o, ln1_g, ln1_b, peer_wq, peer_subkeys, peer_u, peer_v, ln2_g, ln2_b):
    return _forward(x, meta, ln_in_g, ln_in_b, w_in, b_gate, a_kv_norm_g, a_w_uk, a_w_uv, a_kidx_g, a_kidx_b,
                    b_mu, b_w0, b_w2, b_a0, b_a2, b_g2, b_k_k, b_k_a, b_r_k, b_lnx_g, b_lnx_b,
                    w_pa, w_pb, w_o, ln1_g, ln1_b, peer_wq, peer_subkeys, peer_u, peer_v, ln2_g, ln2_b)
```

```python
import functools
import math

import jax
import jax.numpy as jnp
from jax import lax
from jax.experimental import pallas as pl
from jax.experimental.pallas import tpu as pltpu

F32 = jnp.float32
BF16 = jnp.bfloat16
I32 = jnp.int32

D_MODEL = 1024
CHUNK = 64
N_META = 16
Q_BLOCK = 128
A_HEADS = 8
A_HEAD_DIM = 64
A_KV_RANK = 128
IDX_HEADS = 8
IDX_DIM = 64
TOPK_MAX = 256
B_HEADS = 8
B_HEAD_DIM = 64
DECAY_LORA = 64
AAA_LORA = 64
GATE_LORA = 128
B_LNX_EPS = 64e-5
PEER_HEADS = 8
PEER_NKEYS = 128
PEER_HALF = 128
PEER_TOPK = 16
A_WIDTH = A_HEADS * A_HEAD_DIM
B_WIDTH = B_HEADS * B_HEAD_DIM
B_COLS = 3 * B_WIDTH + DECAY_LORA + AAA_LORA + GATE_LORA
LN_EPS = 1e-5
DEPTH = 1

LANES = 128
SUBLANES = 8
VMEM_LIMIT = 56 * 1024 * 1024
PA_Q = 0
PA_CKV = PA_Q + A_WIDTH
PA_MISC = PA_CKV + A_KV_RANK
PA_QI = PA_MISC + LANES
PA_COLS = PA_QI + IDX_HEADS * LANES
NEG_BIG = -1e30


def _cparams(sem):
    return pltpu.CompilerParams(dimension_semantics=sem, vmem_limit_bytes=VMEM_LIMIT)


def _layernorm(x, g, b, eps=LN_EPS):
    mu = jnp.mean(x, axis=-1, keepdims=True)
    xc = x - mu
    var = jnp.mean(xc * xc, axis=-1, keepdims=True)
    return xc * lax.rsqrt(var + eps) * g + b


def _dot(a, b):
    return jnp.dot(a, b, preferred_element_type=F32)


def _dot_nt(a, b):
    return lax.dot_general(a, b, (((1,), (1,)), ((), ())), preferred_element_type=F32)


def _dot_tn(a, b):
    return lax.dot_general(a, b, (((0,), (0,)), ((), ())), preferred_element_type=F32)


def _split2(a):
    hi = a.astype(BF16)
    lo = (a - hi.astype(F32)).astype(BF16)
    return hi, lo


def _split3(a):
    hi = a.astype(BF16)
    r = a - hi.astype(F32)
    mid = r.astype(BF16)
    lo = (r - mid.astype(F32)).astype(BF16)
    return hi, mid, lo


def _mm3(a, b, dot=_dot):
    ah, al = _split2(a)
    bh, bl = _split2(b)
    return dot(ah, bh) + (dot(ah, bl) + dot(al, bh))


def _mm1(a, b, dot=_dot):
    return dot(a.astype(BF16), b.astype(BF16))


def _mm_exact_rhs(a, m, dot=_dot):
    h, mid, lo = _split3(a)
    return dot(h, m) + (dot(mid, m) + dot(lo, m))


def _mm_exact_lhs(m, a, dot=_dot):
    h, mid, lo = _split3(a)
    return dot(m, h) + (dot(m, mid) + dot(m, lo))


def _ln_proj_kernel(x_ref, g_ref, b_ref, wa_ref, wb_ref, wg_ref, pa_ref, pb_ref, pg_ref):
    h = _layernorm(x_ref[...], g_ref[...], b_ref[...]).astype(BF16)
    pa_ref[...] = _dot(h, wa_ref[...])
    pb_ref[...] = _dot(h, wb_ref[...])
    pg_ref[...] = _dot(h, wg_ref[...])


def _ln_proj(x2d, g, b, wa, wb, wg, tm, interpret):
    n, d = x2d.shape
    full = lambda a: pl.BlockSpec(a.shape, lambda i: (0,) * a.ndim)
    row = lambda c: pl.BlockSpec((tm, c), lambda i: (i, 0))
    return pl.pallas_call(
        _ln_proj_kernel,
        grid=(n // tm,),
        in_specs=[row(d), full(g), full(b), full(wa), full(wb), full(wg)],
        out_specs=[row(wa.shape[1]), row(wb.shape[1]), row(wg.shape[1])],
        out_shape=[jax.ShapeDtypeStruct((n, w.shape[1]), F32) for w in (wa, wb, wg)],
        compiler_params=_cparams(("parallel",)),
        interpret=interpret,
        name="ln_proj",
    )(x2d, g, b, wa, wb, wg)


def _kv_prep_kernel(p_ref, kvg_ref, kig_ref, kib_ref, ckv_ref, kidx_ref):
    blk = p_ref[...]
    c = blk[:, :LANES]
    cn = c * lax.rsqrt(jnp.mean(c * c, axis=-1, keepdims=True) + 1e-6) * kvg_ref[...]
    ckv_ref[...] = cn.astype(BF16)
    kx = blk[:, LANES:]
    valid = lax.broadcasted_iota(I32, kx.shape, 1) < IDX_DIM
    mu = jnp.sum(jnp.where(valid, kx, 0.0), axis=-1, keepdims=True) * (1.0 / IDX_DIM)
    dlt = jnp.where(valid, kx - mu, 0.0)
    var = jnp.sum(dlt * dlt, axis=-1, keepdims=True) * (1.0 / IDX_DIM)
    kn = dlt * lax.rsqrt(var + LN_EPS) * kig_ref[...] + kib_ref[...]
    kidx_ref[...] = jnp.where(valid, kn, 0.0).astype(BF16)


def _kv_prep(pa, kvg, kig, kib, tm, interpret):
    n = pa.shape[0]
    full = lambda a: pl.BlockSpec(a.shape, lambda i: (0,) * a.ndim)
    return pl.pallas_call(
        _kv_prep_kernel,
        grid=(n // tm,),
        in_specs=[pl.BlockSpec((tm, 2 * LANES), lambda i: (i, PA_CKV // (2 * LANES))),
                  full(kvg), full(kig), full(kib)],
        out_specs=[pl.BlockSpec((tm, LANES), lambda i: (i, 0)),
                   pl.BlockSpec((tm, LANES), lambda i: (i, 0))],
        out_shape=[jax.ShapeDtypeStruct((n, LANES), BF16),
                   jax.ShapeDtypeStruct((n, LANES), BF16)],
        compiler_params=_cparams(("parallel",)),
        interpret=interpret,
        name="dsa_kv_prep",
    )(pa, kvg, kig, kib)


def _fold_rows(x, op):
    while x.shape[0] > SUBLANES:
        half = x.shape[0] // 2
        x = op(x[:half], x[half:])
    return x


def _dsa_kernel(pa_ref, kidx_ref, ckv_ref, ckvt_ref, wuk_ref, wuv_ref, tril_ref, o_ref,
                sc_ref, qabs_ref, qi_ref, wt_ref, m_ref, acc_ref, *, k_sel, kt, w_scale, att_scale):
    qb = pl.program_id(1)
    Q = Q_BLOCK
    R = A_KV_RANK
    npair = A_HEADS // 2
    kb_per_tile = kt // LANES
    nblk = qb + 2
    ntile = (nblk + kb_per_tile - 1) // kb_per_tile

    pa = pa_ref[...]
    q = pa[:, PA_Q:PA_Q + A_WIDTH].astype(BF16)
    for h in range(A_HEADS):
        cols = slice((h % 2) * Q, (h % 2 + 1) * Q)
        qabs_ref[h // 2, :, cols] = (_dot(q, wuk_ref[h]) * att_scale).T.astype(BF16)
        qi_ref[h // 2, :, cols] = pa[:, PA_QI + h * LANES: PA_QI + (h + 1) * LANES].T.astype(BF16)
    wt_ref[...] = (pa[:, PA_MISC:PA_MISC + LANES] * w_scale).T

    key_row = lax.broadcasted_iota(I32, (kt, Q), 0)
    qry = lax.broadcasted_iota(I32, (1, Q), 1)
    rowlim = LANES + Q * qb + CHUNK * (qry // CHUNK + 1)

    def score_tile(t, carry):
        hi, lo = carry
        k0 = pl.multiple_of(t * kt, kt)
        kblk = kidx_ref[pl.ds(k0, kt), :]
        logits = [_dot(kblk, qi_ref[hp]) for hp in range(npair)]
        parts = []
        for hp, logit in enumerate(logits):
            logit = jnp.maximum(logit, 0.0)
            w0 = wt_ref[IDX_DIM + 2 * hp:IDX_DIM + 2 * hp + 1, :]
            w1 = wt_ref[IDX_DIM + 2 * hp + 1:IDX_DIM + 2 * hp + 2, :]
            parts.append(logit[:, :Q] * w0 + logit[:, Q:] * w1)
        score = (parts[0] + parts[1]) + (parts[2] + parts[3])
        g = key_row + k0
        adm = (g < N_META) | ((g >= LANES) & (g < rowlim))
        sc_ref[t] = jnp.where(adm, score, -jnp.inf)
        hi = jnp.maximum(hi, _fold_rows(jnp.where(adm, score, -jnp.inf), jnp.maximum))
        lo = jnp.minimum(lo, _fold_rows(jnp.where(adm, score, jnp.inf), jnp.minimum))
        return hi, lo

    hi8, lo8 = lax.fori_loop(0, ntile, score_tile, (jnp.full((SUBLANES, Q), -jnp.inf, F32),
                                                     jnp.full((SUBLANES, Q), jnp.inf, F32)))
    row_max = jnp.max(hi8, axis=0, keepdims=True)
    row_min = jnp.min(lo8, axis=0, keepdims=True)

    def count_ge(cand):
        def body(t, acc):
            hit = jnp.where(sc_ref[t] >= cand, 1.0, 0.0)
            return acc + _fold_rows(hit, jnp.add)

        acc = lax.fori_loop(0, ntile, body, jnp.zeros((SUBLANES, Q), F32))
        return jnp.sum(acc, axis=0, keepdims=True)

    def next_below(bound):
        def body(t, acc):
            sc = sc_ref[t]
            return jnp.maximum(acc, _fold_rows(jnp.where(sc < bound, sc, -jnp.inf), jnp.maximum))

        acc = lax.fori_loop(0, ntile, body, jnp.full((SUBLANES, Q), -jnp.inf, F32))
        return jnp.max(acc, axis=0, keepdims=True)

    kf = float(k_sel)
    take_all = 1e9
    n_adm = (N_META + Q * qb + CHUNK * (qry // CHUNK + 1)).astype(F32)
    c_max = count_ge(row_max)
    few = n_adm <= kf
    top_ties = jnp.logical_not(few) & (c_max >= kf)
    done = jnp.where(few | top_ties, 1.0, 0.0)
    thr = jnp.where(few, row_min, row_max)
    need = jnp.where(few, take_all, kf)
    n_eq = jnp.where(few, 0.0, c_max)

    def any_set(flag):
        return (jnp.max(jnp.where(flag, 1.0, 0.0)) > 0.0).astype(I32)

    def bisect(state):
        lo, hi, c_hi, thr, need, n_eq, done = state
        mid = lo + 0.5 * (hi - lo)
        live = (done < 0.5) & (mid > lo) & (mid < hi)
        c = count_ge(mid)
        found = live & (c == kf)
        up = live & (c > kf)
        down = live & (c < kf)
        state = (jnp.where(up, mid, lo), jnp.where(down, mid, hi), jnp.where(down, c, c_hi),
                 jnp.where(found, mid, thr), jnp.where(found, take_all, need), jnp.where(found, 0.0, n_eq),
                 jnp.where(found, 1.0, done))
        return state, up | down

    def bisect_twice(carry):
        state, _ = bisect(carry[:-1])
        state, moved = bisect(state)
        return state + (any_set(moved),)

    state = (row_min, row_max, c_max, thr, need, n_eq, done)
    state = lax.fori_loop(0, 12, lambda i, s: bisect(s)[0], state)
    state = lax.while_loop(lambda s: s[-1] > 0, bisect_twice, state + (any_set(done < 0.5),))
    _, hi, c_hi, thr, need, n_eq, done, _ = state

    def step_down(state):
        hi, c_hi, thr, need, n_eq, done, _ = state
        cand = next_below(hi)
        c = count_ge(cand)
        fin = (done < 0.5) & (c >= kf)
        go = (done < 0.5) & (c < kf)
        state = (jnp.where(go, cand, hi), jnp.where(go, c, c_hi), jnp.where(fin, cand, thr),
                 jnp.where(fin, kf - c_hi, need), jnp.where(fin, c - c_hi, n_eq), jnp.where(fin, 1.0, done))
        return state + (any_set(go),)

    state = (hi, c_hi, thr, need, n_eq, done, any_set(done < 0.5))
    _, _, thr, need, n_eq, _, _ = lax.while_loop(lambda s: s[-1] > 0, step_down, state)

    @pl.when(any_set(n_eq > need) > 0)
    def _():
        def demote(t, eq_before):
            sc = sc_ref[t]
            eq = sc == thr
            eqf = jnp.where(eq, 1.0, 0.0)
            prefix = _dot(tril_ref[...], eqf.astype(BF16))
            keep = (prefix + eq_before) < need
            sc_ref[t] = jnp.where(eq & jnp.logical_not(keep), -jnp.inf, sc)
            return eq_before + jnp.sum(eqf, axis=0, keepdims=True)

        lax.fori_loop(0, ntile, demote, jnp.zeros((1, Q), F32))

    m_ref[...] = jnp.full(m_ref.shape, NEG_BIG, F32)
    acc_ref[...] = jnp.zeros(acc_ref.shape, F32)

    def attn_tile(t, carry):
        k0 = pl.multiple_of(t * kt, kt)
        bias = jnp.where(sc_ref[t] >= thr, 0.0, NEG_BIG)
        bias2 = jnp.concatenate([bias, bias], axis=1)
        ck = ckv_ref[pl.ds(k0, kt), :]
        ckt = ckvt_ref[:, pl.ds(k0, kt)]
        for g0 in range(0, npair, 2):
            grp = range(g0, g0 + 2)
            ss = [_dot(ck, qabs_ref[hp]) + bias2 for hp in grp]
            m_prev = [m_ref[hp] for hp in grp]
            m_new = [jnp.maximum(mp, jnp.max(_fold_rows(s, jnp.maximum), axis=0, keepdims=True))
                     for mp, s in zip(m_prev, ss)]
            ps = [jnp.exp(s - mn[0:1]).astype(BF16) for s, mn in zip(ss, m_new)]
            pv = [_dot(ckt, p) for p in ps]
            for i, hp in enumerate(grp):
                acc_ref[hp] = acc_ref[hp] * jnp.exp(m_prev[i][0:1] - m_new[i][0:1]) + pv[i]
                m_ref[hp] = m_new[i]
        return carry

    lax.fori_loop(0, ntile, attn_tile, 0)

    out = jnp.zeros((Q, A_WIDTH), F32)
    for hp in range(npair):
        acc = acc_ref[hp]
        o_lat_t = acc[:R, :] / acc[R:R + 1, :]
        for j in range(2):
            o_lat = o_lat_t[:, j * Q:(j + 1) * Q].T.astype(BF16)
            out = out + _dot(o_lat, wuv_ref[2 * hp + j])
    o_ref[...] = out.astype(BF16)


def _dsa(pa3, kidx_keys, ckv_keys, ckvt_keys, wuk_pad, wuv_pad, k_sel, kt, interpret):
    bn, s, _ = pa3.shape
    lk = kidx_keys.shape[1]
    nqb = s // Q_BLOCK
    ntile_max = lk // kt
    npair = A_HEADS // 2
    rt = ckvt_keys.shape[1]
    tril = jnp.tril(jnp.ones((kt, kt), F32), k=-1).astype(BF16)
    full = lambda a: pl.BlockSpec(a.shape, lambda b, q: (0,) * a.ndim)
    kern = functools.partial(_dsa_kernel, k_sel=k_sel, kt=kt,
                             w_scale=(IDX_HEADS ** -0.5) * (IDX_DIM ** -0.5), att_scale=A_HEAD_DIM ** -0.5)
    return pl.pallas_call(
        kern,
        grid=(bn, nqb),
        in_specs=[pl.BlockSpec((None, Q_BLOCK, PA_COLS), lambda b, q: (b, q, 0)),
                  pl.BlockSpec((None, lk, LANES), lambda b, q: (b, 0, 0)),
                  pl.BlockSpec((None, lk, A_KV_RANK), lambda b, q: (b, 0, 0)),
                  pl.BlockSpec((None, rt, lk), lambda b, q: (b, 0, 0)),
                  full(wuk_pad), full(wuv_pad), full(tril)],
        out_specs=pl.BlockSpec((None, Q_BLOCK, A_WIDTH), lambda b, q: (b, q, 0)),
        out_shape=jax.ShapeDtypeStruct((bn, s, A_WIDTH), BF16),
        scratch_shapes=[pltpu.VMEM((ntile_max, kt, Q_BLOCK), F32),
                        pltpu.VMEM((npair, A_KV_RANK, 2 * Q_BLOCK), BF16),
                        pltpu.VMEM((npair, LANES, 2 * Q_BLOCK), BF16),
                        pltpu.VMEM((LANES, Q_BLOCK), F32),
                        pltpu.VMEM((npair, SUBLANES, 2 * Q_BLOCK), F32),
                        pltpu.VMEM((npair, rt, 2 * Q_BLOCK), F32)],
        compiler_params=_cparams(("parallel", "arbitrary")),
        interpret=interpret,
        name="dsa_attention",
    )(pa3, kidx_keys, ckv_keys, ckvt_keys, wuk_pad, wuv_pad, tril)


def _rwkv_prep_kernel(pb_ref, prev0_ref, mu_ref, w0_ref, a0_ref, kk_ref, ka_ref, w2a2_ref, g2_ref, hsum_ref,
                      r_ref, lw_ref, k_ref, v_ref, an_ref, b_ref, g_ref, carry_ref):
    t = pl.program_id(1)

    @pl.when(t == 0)
    def _():
        carry_ref[...] = prev0_ref[...]

    p = pb_ref[...]
    tb = p.shape[0]
    rolled = pltpu.roll(p, 1, 0)
    first = lax.broadcasted_iota(I32, p.shape, 0) == 0
    prev = jnp.where(first, jnp.broadcast_to(carry_ref[0:1, :], p.shape), rolled)
    carry_ref[0:1, :] = p[tb - 1:tb, :]
    xs = p + (prev - p) * mu_ref[...]
    W = B_WIDTH
    r, k, v = xs[:, :W], xs[:, W:2 * W], xs[:, 2 * W:3 * W]
    lora = xs[:, 3 * W:3 * W + LANES]
    lo_lane = lax.broadcasted_iota(I32, lora.shape, 1) < DECAY_LORA
    lora = jnp.where(lo_lane, jnp.tanh(lora), lora).astype(BF16)
    wa = _dot(lora, w2a2_ref[...])
    z = w0_ref[...] + wa[:, :W]
    w_log = -jax.nn.softplus(-z) - 0.5
    a = jax.nn.sigmoid(a0_ref[...] + wa[:, W:])
    gl = xs[:, 3 * W + LANES:]
    g = _dot(jax.nn.sigmoid(gl).astype(BF16), g2_ref[...])
    kk = k * kk_ref[...]
    ss = _mm_exact_rhs(kk * kk, hsum_ref[...])
    kk = kk * lax.rsqrt(ss + 1e-12)
    r_ref[...] = r
    lw_ref[...] = -jnp.exp(w_log)
    k_ref[...] = k * (1.0 + (a - 1.0) * ka_ref[...])
    v_ref[...] = v
    an_ref[...] = -kk
    b_ref[...] = kk * a
    g_ref[...] = g


def _rwkv_prep(pb3, prev0, mu, w0, a0, k_k, k_a, w2a2, g2, hsum, tb, interpret):
    bn, t, _ = pb3.shape
    full = lambda a: pl.BlockSpec(a.shape, lambda b, i: (0,) * a.ndim)
    outspec = pl.BlockSpec((None, tb, B_WIDTH), lambda b, i: (b, i, 0))
    return pl.pallas_call(
        _rwkv_prep_kernel,
        grid=(bn, t // tb),
        in_specs=[pl.BlockSpec((None, tb, B_COLS), lambda b, i: (b, i, 0)),
                  full(prev0), full(mu), full(w0), full(a0), full(k_k), full(k_a), full(w2a2), full(g2),
                  full(hsum)],
        out_specs=[outspec] * 7,
        out_shape=[jax.ShapeDtypeStruct((bn, t, B_WIDTH), F32)] * 7,
        scratch_shapes=[pltpu.VMEM((8, B_COLS), F32)],
        compiler_params=_cparams(("parallel", "arbitrary")),
        interpret=interpret,
        name="rwkv_prep",
    )(pb3, prev0, mu, w0, a0, k_k, k_a, w2a2, g2, hsum)


def _rwkv_scan_kernel(r_ref, lw_ref, k_ref, v_ref, an_ref, b_ref, s0_ref, ltri_ref, mstrict_ref, mincl_ref,
                      eye_ref, y_ref, sfin_ref, state_ref, *, n_batch, c):
    ci = pl.program_id(0)
    npair = B_HEADS // 2

    @pl.when(ci == 0)
    def _():
        state_ref[...] = s0_ref[...]

    lane = lax.broadcasted_iota(I32, (c, LANES), 1)
    left = lane < B_HEAD_DIM
    mstrict = mstrict_ref[...] > 0.5
    mincl = mincl_ref[...] > 0.5
    eye = eye_ref[...]

    def blockdiag(x):
        return jnp.concatenate([jnp.where(left, x, 0.0), jnp.where(left, 0.0, x)], axis=0)

    pairs = [(bi, pj) for bi in range(n_batch) for pj in range(npair)]
    at2, rt2, bte, kte, v2 = [], [], [], [], []
    a_ab, a_ak, a_rb, a_rk, p_end = [], [], [], [], []
    for bi, pj in pairs:
        ls = slice(pj * LANES, (pj + 1) * LANES)
        lw = lw_ref[bi, :, ls]
        cl = _mm_exact_lhs(ltri_ref[...], lw)
        p_inc = jnp.exp(cl)
        p_inv = jnp.exp(-cl)
        pe = p_inc[c - 1:c, :]
        a2 = blockdiag(an_ref[bi, :, ls] * jnp.exp(cl - lw))
        r2 = blockdiag(r_ref[bi, :, ls] * p_inc)
        b2 = blockdiag(b_ref[bi, :, ls] * p_inv)
        k2 = blockdiag(k_ref[bi, :, ls] * p_inv)
        quad = _mm1(jnp.concatenate([a2, r2], axis=0), jnp.concatenate([b2, k2], axis=0), _dot_nt)
        a_ab.append(jnp.where(mstrict, quad[:2 * c, :2 * c], 0.0))
        a_ak.append(jnp.where(mstrict, quad[:2 * c, 2 * c:], 0.0))
        a_rb.append(jnp.where(mincl, quad[2 * c:, :2 * c], 0.0))
        a_rk.append(jnp.where(mincl, quad[2 * c:, 2 * c:], 0.0))
        at2.append(a2)
        rt2.append(r2)
        bte.append(b2 * pe)
        kte.append(k2 * pe)
        v2.append(blockdiag(v_ref[bi, :, ls]))
        p_end.append(pe)

    tinv = [eye + m for m in a_ab]
    mpow = a_ab
    for _ in range(int(math.log2(c)) - 1):
        mpow = [_mm1(m, m) for m in mpow]
        tinv = [t + _mm1(t, m) for t, m in zip(tinv, mpow)]

    av = [_mm1(a, v) for a, v in zip(a_ak, v2)]
    tatv = [_mm1(t, jnp.concatenate([a, x], axis=1)) for t, a, x in zip(tinv, at2, av)]
    rb = [_mm1(a, x) for a, x in zip(a_rb, tatv)]
    hb = [_mm1(b, x, _dot_tn) for b, x in zip(bte, tatv)]
    rkv = [_mm1(a, v) for a, v in zip(a_rk, v2)]
    kv = [_mm1(k, v, _dot_tn) for k, v in zip(kte, v2)]

    for i, (bi, pj) in enumerate(pairs):
        ls = slice(pj * LANES, (pj + 1) * LANES)
        ra = rt2[i] + rb[i][:, :LANES]
        ha = eye * p_end[i] + hb[i][:, :LANES]
        hbd = state_ref[bi * npair + pj]
        nxt = _mm3(jnp.concatenate([ra, ha], axis=0), hbd)
        y2 = nxt[:2 * c] + rb[i][:, LANES:] + rkv[i]
        y_ref[bi, :, ls] = y2[:c] + y2[c:]
        state_ref[bi * npair + pj] = nxt[2 * c:] + hb[i][:, LANES:] + kv[i]

    @pl.when(ci == pl.num_programs(0) - 1)
    def _():
        sfin_ref[...] = state_ref[...]


def _rwkv_scan(r, lw, k, v, an, b, s0, c, interpret):
    bn, t, _ = r.shape
    npair = B_HEADS // 2
    idx = jnp.arange(2 * c)
    same = (idx[:, None] // c) == (idx[None, :] // c)
    mstrict = (same & ((idx[:, None] % c) > (idx[None, :] % c))).astype(F32)
    mincl = (same & ((idx[:, None] % c) >= (idx[None, :] % c))).astype(F32)
    ltri = jnp.tril(jnp.ones((c, c), F32)).astype(BF16)
    eye = jnp.eye(2 * c, dtype=F32)
    assert 2 * c == LANES
    full = lambda a: pl.BlockSpec(a.shape, lambda i: (0,) * a.ndim)
    seq = pl.BlockSpec((bn, c, B_WIDTH), lambda i: (0, i, 0))
    kern = functools.partial(_rwkv_scan_kernel, n_batch=bn, c=c)
    return pl.pallas_call(
        kern,
        grid=(t // c,),
        in_specs=[seq] * 6 + [full(s0), full(ltri), full(mstrict), full(mincl), full(eye)],
        out_specs=[seq, full(s0)],
        out_shape=[jax.ShapeDtypeStruct((bn, t, B_WIDTH), F32),
                   jax.ShapeDtypeStruct(s0.shape, F32)],
        scratch_shapes=[pltpu.VMEM(s0.shape, F32)],
        compiler_params=_cparams(("arbitrary",)),
        interpret=interpret,
        name="rwkv_scan",
    )(r, lw, k, v, an, b, s0, ltri, mstrict, mincl, eye)


def _rwkv_post_kernel(y_ref, r_ref, k_ref, v_ref, g_ref, rk_ref, lg_ref, lb_ref, hsum_ref, o_ref):
    y = y_ref[...]
    hs = hsum_ref[...]
    inv_n = 1.0 / B_HEAD_DIM
    mu = _mm_exact_rhs(y, hs) * inv_n
    d = y - mu
    var = _mm_exact_rhs(d * d, hs) * inv_n
    yn = d * lax.rsqrt(var + B_LNX_EPS) * lg_ref[...] + lb_ref[...]
    bonus = _mm_exact_rhs(r_ref[...] * k_ref[...] * rk_ref[...], hs) * v_ref[...]
    o_ref[...] = ((yn + bonus) * g_ref[...]).astype(BF16)


def _rwkv_post(y, r, k, v, g, r_k, lnx_g, lnx_b, hsum, tm, interpret):
    n = y.shape[0]
    full = lambda a: pl.BlockSpec(a.shape, lambda i: (0,) * a.ndim)
    row = pl.BlockSpec((tm, B_WIDTH), lambda i: (i, 0))
    return pl.pallas_call(
        _rwkv_post_kernel,
        grid=(n // tm,),
        in_specs=[row] * 5 + [full(r_k), full(lnx_g), full(lnx_b), full(hsum)],
        out_specs=row,
        out_shape=jax.ShapeDtypeStruct((n, B_WIDTH), BF16),
        compiler_params=_cparams(("parallel",)),
        interpret=interpret,
        name="rwkv_post",
    )(y, r, k, v, g, r_k, lnx_g, lnx_b, hsum)


def _mix_kernel(x_ref, oa_ref, ob_ref, pg_ref, lg_ref, lb_ref, bg_ref, wpa_ref, wpb_ref, wo_ref,
                l1g_ref, l1b_ref, h_ref, *, alpha):
    h0 = _layernorm(x_ref[...], lg_ref[...], lb_ref[...])
    gates = jax.nn.sigmoid(pg_ref[...] + bg_ref[...])
    mixed = (gates[:, :D_MODEL] * _dot(oa_ref[...], wpa_ref[...])
             + gates[:, D_MODEL:] * _dot(ob_ref[...], wpb_ref[...]))
    pre = alpha * h0 + _dot(mixed.astype(BF16), wo_ref[...])
    h_ref[...] = _layernorm(pre, l1g_ref[...], l1b_ref[...])


def _mix(x2d, oa, ob, pg, lg, lb, bg, wpa, wpb, wo, l1g, l1b, alpha, tm, interpret):
    n = x2d.shape[0]
    full = lambda a: pl.BlockSpec(a.shape, lambda i: (0,) * a.ndim)
    row = lambda c: pl.BlockSpec((tm, c), lambda i: (i, 0))
    return pl.pallas_call(
        functools.partial(_mix_kernel, alpha=alpha),
        grid=(n // tm,),
        in_specs=[row(D_MODEL), row(A_WIDTH), row(B_WIDTH), row(2 * D_MODEL),
                  full(lg), full(lb), full(bg), full(wpa), full(wpb), full(wo), full(l1g), full(l1b)],
        out_specs=row(D_MODEL),
        out_shape=jax.ShapeDtypeStruct((n, D_MODEL), F32),
        compiler_params=_cparams(("parallel",)),
        interpret=interpret,
        name="mix_out_proj",
    )(x2d, oa, ob, pg, lg, lb, bg, wpa, wpb, wo, l1g, l1b)


def _sort16_pairs():
    def merge(lo, hi, r):
        step = r * 2
        if step < hi - lo:
            yield from merge(lo, hi, step)
            yield from merge(lo + r, hi, step)
            yield from [(i, i + r) for i in range(lo + r, hi - r, step)]
        else:
            yield (lo, lo + r)

    def sort(lo, hi):
        if hi - lo >= 1:
            mid = lo + (hi - lo) // 2
            yield from sort(lo, mid)
            yield from sort(mid + 1, hi)
            yield from merge(lo, hi, 1)

    return tuple(sort(0, 15))


_SORT16 = _sort16_pairs()


def _ce(v, i, j):
    hi, lo = jnp.maximum(v[i], v[j]), jnp.minimum(v[i], v[j])
    v[i], v[j] = hi, lo


def _top16_sorted(st):
    v = [st[SUBLANES * k:SUBLANES * (k + 1), :] for k in range(16)]
    for i, j in _SORT16:
        _ce(v, i, j)
    for d in (4, 2, 1):
        w = [pltpu.roll(x, d, 0) for x in v]
        v = [jnp.maximum(v[k], w[15 - k]) for k in range(16)]
        for dist in (8, 4, 2, 1):
            for k in range(16):
                if not k & dist:
                    _ce(v, k, k + dist)
    return v


def _top16_ranked(st):
    x = st
    rank = jnp.full(st.shape, float(PEER_TOPK), F32)
    rows = []
    for m in range(PEER_TOPK):
        mx = jnp.max(x, axis=0, keepdims=True)
        hit = x == mx
        rank = jnp.where(hit, float(m), rank)
        x = jnp.where(hit, -jnp.inf, x)
        rows.append(mx)
    return rows, rank


def _rows_to_sublanes(rows, shape):
    sub = lax.broadcasted_iota(I32, shape, 0)
    out = jnp.broadcast_to(rows[-1], shape)
    for m in range(len(rows) - 2, -1, -1):
        out = jnp.where(sub == m, rows[m], out)
    return out


def _peer_route(s1, s2):
    tb = s1.shape[1]
    shape8 = (SUBLANES, tb)
    a = _top16_sorted(s1)
    b_rows, rank2 = _top16_ranked(s2)
    sub = lax.broadcasted_iota(I32, shape8, 0)
    b_lo = _rows_to_sublanes(b_rows[:8], shape8)
    b_hi = _rows_to_sublanes(b_rows[8:], shape8)
    a_hi = _rows_to_sublanes(a[8:], shape8)
    cands = [a[0] + b_lo, a[0] + b_hi, a[1] + b_lo]
    for i in range(2, 8):
        cands.append(jnp.where(sub < PEER_TOPK // (i + 1), a[i] + b_lo, -jnp.inf))
    cands.append(a_hi + b_rows[0])
    work = cands
    tau = None
    for r in range(PEER_TOPK):
        mx = work[0]
        for c in work[1:]:
            mx = jnp.maximum(mx, c)
        tau = jnp.max(mx, axis=0, keepdims=True)
        if r + 1 < PEER_TOPK:
            work = [jnp.where(c == tau, -jnp.inf, c) for c in work]
    cmax = a[0][0:1] + b_rows[0]
    z = jnp.zeros(shape8, F32)
    for c in cands:
        z = z + jnp.where(c >= tau, jnp.exp(c - cmax), 0.0)
    z = jnp.sum(z, axis=0, keepdims=True)
    count = jnp.zeros(s1.shape, F32)
    for m in range(4):
        count = count + jnp.where((s1 + b_rows[m]) >= tau, 1.0, 0.0)
    for i in range(3):
        extra = jnp.zeros(tau.shape, F32)
        for m in range(4, PEER_TOPK // (i + 1)):
            extra = extra + jnp.where((a[i][0:1] + b_rows[m]) >= tau, 1.0, 0.0)
        count = count + jnp.where(s1 == a[i][0:1], extra, 0.0)
    e1 = jnp.exp(s1 - a[0][0:1]) / z
    e2 = jnp.exp(s2 - b_rows[0])
    return count, e1, rank2, e2


def _peer_kernel(h_ref, wqt_ref, sk_ref, u_ref, vt_ref, l2g_ref, l2b_ref, o_ref,
                 hbt_ref, st_ref, cnt_ref, e1_ref, r2_ref, e2_ref, acc_ref, *, alpha, eb):
    e = pl.program_id(1)
    PH = PEER_HEADS
    NK = PEER_NKEYS
    tb = h_ref.shape[0]
    pack = 2 * SUBLANES

    @pl.when(e == 0)
    def _():
        hbt_ref[...] = h_ref[...].T.astype(BF16)
        acc_ref[...] = jnp.zeros(acc_ref.shape, F32)
        for h in range(PH):
            if h % 2 == 0:
                rows = slice(2 * h * PEER_HALF, (2 * h + 4) * PEER_HALF)
                qt2 = _dot(wqt_ref[rows, :], hbt_ref[...]).astype(BF16)
            for p in range(2):
                hp = 2 * (h % 2) + p
                st_ref[p] = _dot(sk_ref[2 * h + p], qt2[hp * PEER_HALF:(hp + 1) * PEER_HALF])

            def route_tile(lt, carry, h=h):
                sl = pl.ds(pl.multiple_of(lt * LANES, LANES), LANES)
                count, e1, rank2, e2 = _peer_route(st_ref[0, :, sl], st_ref[1, :, sl])
                cnt_ref[h, :, sl] = count
                e1_ref[h, :, sl] = e1
                r2_ref[h, :, :, sl] = rank2.astype(BF16).reshape(NK // pack, pack, LANES)
                e2_ref[h, :, :, sl] = e2.astype(BF16).reshape(NK // pack, pack, LANES)
                return carry

            lax.fori_loop(0, tb // LANES, route_tile, 0)

    n_i = eb // NK
    gates = []
    for ii in range(n_i):
        irow = e * n_i + ii
        gi = jnp.zeros((NK // pack, pack, tb), BF16)
        for h in range(PH):
            cb = jnp.broadcast_to(cnt_ref[h, pl.ds(irow, 1), :], (pack, tb)).astype(BF16)
            e1b = jnp.broadcast_to(e1_ref[h, pl.ds(irow, 1), :], (pack, tb)).astype(BF16)
            hit = r2_ref[h] < cb[None]
            gi = gi + jnp.where(hit, e2_ref[h] * e1b[None], jnp.zeros((), BF16))
        gates.append(gi.reshape(NK, tb))
    ht = _dot(u_ref[...], hbt_ref[...])
    act = (0.5 * ht * (1.0 + lax.erf(ht * (2.0 ** -0.5)))).astype(BF16)
    coef = jnp.concatenate(gates, axis=0) * act
    acc_ref[...] += _dot(vt_ref[...], coef)

    @pl.when(e == pl.num_programs(1) - 1)
    def _():
        pre = alpha * h_ref[...] + acc_ref[...].T
        o_ref[...] = _layernorm(pre, l2g_ref[...], l2b_ref[...])


def _peer(h1, wqt, subkeys, u_bf, vt_bf, l2g, l2b, alpha, tb, eb, interpret):
    n = h1.shape[0]
    ne = u_bf.shape[0]
    pack = 2 * SUBLANES
    full = lambda a: pl.BlockSpec(a.shape, lambda i, e: (0,) * a.ndim, pipeline_mode=pl.Buffered(1))
    return pl.pallas_call(
        functools.partial(_peer_kernel, alpha=alpha, eb=eb),
        grid=(n // tb, ne // eb),
        in_specs=[pl.BlockSpec((tb, D_MODEL), lambda i, e: (i, 0), pipeline_mode=pl.Buffered(1)),
                  full(wqt), full(subkeys),
                  pl.BlockSpec((eb, D_MODEL), lambda i, e: (e, 0)),
                  pl.BlockSpec((None, D_MODEL, eb), lambda i, e: (e, 0, 0)),
                  full(l2g), full(l2b)],
        out_specs=pl.BlockSpec((tb, D_MODEL), lambda i, e: (i, 0)),
        out_shape=jax.ShapeDtypeStruct((n, D_MODEL), F32),
        scratch_shapes=[pltpu.VMEM((D_MODEL, tb), BF16),
                        pltpu.VMEM((2, PEER_NKEYS, tb), F32),
                        pltpu.VMEM((PEER_HEADS, PEER_NKEYS, tb), F32),
                        pltpu.VMEM((PEER_HEADS, PEER_NKEYS, tb), F32),
                        pltpu.VMEM((PEER_HEADS, PEER_NKEYS // pack, pack, tb), BF16),
                        pltpu.VMEM((PEER_HEADS, PEER_NKEYS // pack, pack, tb), BF16),
                        pltpu.VMEM((D_MODEL, tb), F32)],
        compiler_params=_cparams(("parallel", "arbitrary")),
        interpret=interpret,
        name="peer",
    )(h1, wqt, subkeys, u_bf, vt_bf, l2g, l2b)


def _row(v):
    return v.reshape(1, -1).astype(F32)


def _forward(x, meta, ln_in_g, ln_in_b, w_in, b_gate, a_kv_norm_g, a_w_uk, a_w_uv, a_kidx_g, a_kidx_b,
             b_mu, b_w0, b_w2, b_a0, b_a2, b_g2, b_k_k, b_k_a, b_r_k, b_lnx_g, b_lnx_b,
             w_pa, w_pb, w_o, ln1_g, ln1_b, peer_wq, peer_subkeys, peer_u, peer_v, ln2_g, ln2_b,
             interpret=False, tm=256, kt=512, peer_tb=1024, peer_eb=1024):
    bn, s, d = x.shape
    assert s % Q_BLOCK == 0 and d == D_MODEL
    n = bn * s
    k_sel = min(TOPK_MAX, s // 4)
    alpha = (2.0 * DEPTH) ** 0.25
    l = 0
    tm = min(tm, n)

    w = w_in[l]
    o = 0
    aq = w[:, o:o + A_WIDTH]; o += A_WIDTH
    ackv = w[:, o:o + A_KV_RANK]; o += A_KV_RANK
    aqi = w[:, o:o + IDX_HEADS * IDX_DIM]; o += IDX_HEADS * IDX_DIM
    aki = w[:, o:o + IDX_DIM]; o += IDX_DIM
    awi = w[:, o:o + IDX_HEADS]; o += IDX_HEADS
    wbc = w[:, o:o + B_COLS]; o += B_COLS
    wgc = w[:, o:o + 2 * D_MODEL]
    misc = jnp.concatenate([aki, awi, jnp.zeros((d, LANES - IDX_DIM - IDX_HEADS), F32)], axis=1)
    aqi_p = jnp.pad(aqi.reshape(d, IDX_HEADS, IDX_DIM), ((0, 0), (0, 0), (0, LANES - IDX_DIM)))
    wa = jnp.concatenate([aq, ackv, misc, aqi_p.reshape(d, IDX_HEADS * LANES)], axis=1).astype(BF16)
    wb = wbc.astype(BF16)
    wg = wgc.astype(BF16)

    wuk = jnp.transpose(a_w_uk[l], (1, 2, 0))
    wuk_pad = jnp.zeros((A_HEADS, A_HEADS, A_HEAD_DIM, A_KV_RANK), F32)
    wuk_pad = wuk_pad.at[jnp.arange(A_HEADS), jnp.arange(A_HEADS)].set(wuk)
    wuk_pad = wuk_pad.reshape(A_HEADS, A_WIDTH, A_KV_RANK).astype(BF16)
    wuv = jnp.transpose(a_w_uv[l], (1, 0, 2))
    wuv_pad = jnp.zeros((A_HEADS, A_KV_RANK, A_HEADS, A_HEAD_DIM), F32)
    wuv_pad = wuv_pad.at[jnp.arange(A_HEADS), :, jnp.arange(A_HEADS)].set(wuv)
    wuv_pad = wuv_pad.reshape(A_HEADS, A_KV_RANK, A_WIDTH).astype(BF16)
    kig = jnp.pad(a_kidx_g[l], (0, LANES - IDX_DIM)).reshape(1, LANES)
    kib = jnp.pad(a_kidx_b[l], (0, LANES - IDX_DIM)).reshape(1, LANES)

    w2a2 = jnp.zeros((LANES, 2 * B_WIDTH), F32)
    w2a2 = w2a2.at[:DECAY_LORA, :B_WIDTH].set(b_w2[l]).at[DECAY_LORA:, B_WIDTH:].set(b_a2[l]).astype(BF16)
    hid = jnp.arange(B_WIDTH) // B_HEAD_DIM
    hsum = (hid[:, None] == hid[None, :]).astype(BF16)

    lg, lb = _row(ln_in_g), _row(ln_in_b)

    x2d = x.reshape(n, d)
    pa, pb, pg = _ln_proj(x2d, lg, lb, wa, wb, wg, tm, interpret)
    pa_m, pb_m, _ = _ln_proj(meta.astype(F32), lg, lb, wa, wb, wg, N_META, interpret)

    kvg = _row(a_kv_norm_g[l])
    ckv, kidx = _kv_prep(pa, kvg, kig, kib, tm, interpret)
    ckv_m, kidx_m = _kv_prep(pa_m, kvg, kig, kib, N_META, interpret)
    nblk_max = s // Q_BLOCK + 1
    lk = -(-nblk_max * LANES // kt) * kt

    def keys(meta_rows, frame_rows):
        c = frame_rows.shape[-1]
        fr = frame_rows.reshape(bn, s, c)
        mt = jnp.broadcast_to(meta_rows[None], (bn, N_META, c))
        return jnp.concatenate([mt, jnp.zeros((bn, LANES - N_META, c), BF16), fr,
                                jnp.zeros((bn, lk - LANES - s, c), BF16)], axis=1)

    ckv_keys = keys(ckv_m, ckv)
    ckvt_keys = jnp.concatenate([jnp.swapaxes(ckv_keys, 1, 2), jnp.ones((bn, 1, lk), BF16),
                                 jnp.zeros((bn, 2 * SUBLANES - 1, lk), BF16)], axis=1)
    o_a = _dsa(pa.reshape(bn, s, PA_COLS), keys(kidx_m, kidx), ckv_keys, ckvt_keys, wuk_pad, wuv_pad,
               k_sel, kt, interpret)

    mu, w0, a0 = _row(b_mu[l]), _row(b_w0[l]), _row(b_a0[l])
    k_k, k_a = _row(b_k_k[l]), _row(b_k_a[l])
    g2 = b_g2[l].astype(BF16)
    c = CHUNK
    zero_prev = jnp.zeros((8, B_COLS), F32)
    mr, mlw, mk, mv, man, mb, _ = _rwkv_prep(pb_m.reshape(1, N_META, B_COLS), zero_prev, mu, w0, a0, k_k, k_a,
                                             w2a2, g2, hsum, N_META, interpret)
    padm = lambda a: jnp.pad(a, ((0, 0), (c - N_META, 0), (0, 0)))
    s_zero = jnp.zeros((B_HEADS // 2, LANES, LANES), F32)
    _, s_meta = _rwkv_scan(padm(mr), padm(mlw), padm(mk), padm(mv), padm(man), padm(mb), s_zero, c, interpret)
    prev0 = jnp.broadcast_to(pb_m[N_META - 1:N_META], (8, B_COLS))
    tb_prep = min(tm, s)
    r, lw, k, v, an, b, g = _rwkv_prep(pb.reshape(bn, s, B_COLS), prev0, mu, w0, a0, k_k, k_a, w2a2, g2, hsum,
                                       tb_prep, interpret)
    y, _ = _rwkv_scan(r, lw, k, v, an, b, jnp.concatenate([s_meta] * bn, axis=0), c, interpret)
    f2 = lambda a: a.reshape(n, B_WIDTH)
    o_b = _rwkv_post(f2(y), f2(r), f2(k), f2(v), f2(g), _row(b_r_k[l]), _row(b_lnx_g[l]), _row(b_lnx_b[l]),
                     hsum, tm, interpret)

    h1 = _mix(x2d, o_a.reshape(n, A_WIDTH), o_b, pg, lg, lb, _row(b_gate[l]),
              w_pa[l].astype(BF16), w_pb[l].astype(BF16), w_o[l].astype(BF16),
              _row(ln1_g[l]), _row(ln1_b[l]), alpha, tm, interpret)

    wqt = peer_wq[l].T.astype(BF16)
    subk = peer_subkeys[l].reshape(PEER_HEADS * 2, PEER_NKEYS, PEER_HALF).astype(BF16)
    u_bf = peer_u[l].astype(BF16)
    vt_bf = jnp.swapaxes(peer_v[l].astype(BF16).reshape(-1, peer_eb, d), 1, 2)
    out = _peer(h1, wqt, subk, u_bf, vt_bf, _row(ln2_g[l]), _row(ln2_b[l]), alpha,
                min(peer_tb, n), peer_eb, interpret)
    return out.reshape(bn, s, d)


def kernel(x, meta, ln_in_g, ln_in_b, w_in, b_gate, a_kv_norm_g, a_w_uk, a_w_uv, a_kidx_g, a_kidx_b, b_mu, b_w0, b_w2, b_a0, b_a2, b_g2, b_k_k, b_k_a, b_r_k, b_lnx_g, b_lnx_b, w_pa, w_pb, w_o, ln1_g, ln1_b, peer_wq, peer_subkeys, peer_u, peer_v, ln2_g, ln2_b):
    return _forward(x, meta, ln_in_g, ln_in_b, w_in, b_gate, a_kv_norm_g, a_w_uk, a_w_uv, a_kidx_g, a_kidx_b,
                    b_mu, b_w0, b_w2, b_a0, b_a2, b_g2, b_k_k, b_k_a, b_r_k, b_lnx_g, b_lnx_b,
                    w_pa, w_pb, w_o, ln1_g, ln1_b, peer_wq, peer_subkeys, peer_u, peer_v, ln2_g, ln2_b)
```
